```python
import jax, jax.numpy as jnp
from jax import lax
import numpy as np

D_MODEL = 4096
BATCH = 1
SEQ = 8192
DEPTH = 4

MIX_WIDTH = 2 * D_MODEL
ATTN_HEAD_DIM = 128
ATTN_WIDTH = 3 * MIX_WIDTH // 4
ATTN_HEADS = ATTN_WIDTH // ATTN_HEAD_DIM
DILATED_PATTERNS = ((128, 1), (512, 4), (2048, 16))
POOL_WIDTH = MIX_WIDTH - ATTN_WIDTH
POOL_WINDOWS = (2, 4, 8, 16)
POOL_GROUPS = len(POOL_WINDOWS)
POOL_GROUP_DIM = POOL_WIDTH // POOL_GROUPS
EVEN_IN_WIDTH = 3 * ATTN_WIDTH + POOL_WIDTH + MIX_WIDTH
REL_BUCKETS = 32
REL_MAX_DIST = 2048
MLSTM_WIDTH = 2 * D_MODEL
MLSTM_HEADS = 8
MLSTM_HEAD_DIM = MLSTM_WIDTH // MLSTM_HEADS
MLSTM_CHUNK = 128
CONV_WIDTH = 4
QKV_BLOCK = 4
ODD_IN_WIDTH = 3 * MLSTM_WIDTH
NORM_EPS = 1e-6
N_EVEN = (DEPTH + 1) // 2
N_ODD = DEPTH // 2

kernel_name = 'hybrid_dilated_pool_mlstm_trunk'


def _rms_norm(x, gain):
    xf = x.astype(jnp.float32)
    y = xf * lax.rsqrt(jnp.mean(xf * xf, axis=-1, keepdims=True) + NORM_EPS)
    return (y * gain.astype(jnp.float32)).astype(x.dtype)


def _t5_bucket(dist):
    max_exact = REL_BUCKETS // 2
    safe = np.maximum(dist, 1).astype(np.float32)
    large = max_exact + (np.log(safe / max_exact) / np.log(REL_MAX_DIST / max_exact)
                         * (REL_BUCKETS - max_exact)).astype(np.int32)
    large = np.minimum(large, REL_BUCKETS - 1)
    return np.where(dist < max_exact, dist, large).astype(np.int32)


def _band_structure(window, dilation, n_blocks):
    span = window // dilation
    blk = span
    i = np.arange(blk)[:, None]
    j = np.arange(2 * blk)[None, :]
    rel = i + blk - j
    in_band = (rel >= 0) & (rel <= span)
    prev_ok = (np.arange(n_blocks)[:, None, None] > 0) | (j[None] >= blk)
    mask = in_band[None] & prev_ok
    bucket = _t5_bucket(np.clip(rel, 0, None) * dilation)
    return mask, bucket


def _to_dilated_blocks(t, dilation, blk):
    B, S, H, Dh = t.shape
    unit = dilation * blk
    s_pad = -(-S // unit) * unit
    t = jnp.pad(t, ((0, 0), (0, s_pad - S), (0, 0), (0, 0)))
    t = t.reshape(B, s_pad // dilation, dilation, H, Dh).transpose(0, 2, 3, 1, 4)
    return t.reshape(B, dilation, H, s_pad // unit, blk, Dh)


def _from_dilated_blocks(t, seq):
    B, d, H, nb, L = t.shape[:5]
    rest = t.shape[5:]
    t = jnp.moveaxis(t.reshape((B, d, H, nb * L) + rest), 3, 1)
    return t.reshape((B, nb * L * d, H) + rest)[:, :seq]


def _dilated_window_attention(q, k, v, rel_bias, window, dilation):
    B, S, H, Dh = q.shape
    blk = window // dilation
    qb = _to_dilated_blocks(q, dilation, blk)
    kb = _to_dilated_blocks(k, dilation, blk)
    vb = _to_dilated_blocks(v, dilation, blk)
    n_blocks = qb.shape[3]
    pad_prev = ((0, 0), (0, 0), (0, 0), (1, 0), (0, 0), (0, 0))
    kk = jnp.concatenate([jnp.pad(kb[:, :, :, :-1], pad_prev), kb], axis=4)
    vv = jnp.concatenate([jnp.pad(vb[:, :, :, :-1], pad_prev), vb], axis=4)
    mask, bucket = _band_structure(window, dilation, n_blocks)
    bias = jnp.transpose(rel_bias.astype(jnp.float32)[bucket], (2, 0, 1))
    scores = jnp.einsum('brhcid,brhcjd->brhcij', qb, kk).astype(jnp.float32) + bias[None, None, :, None]
    scores = jnp.where(mask[None, None, None], scores, -jnp.inf)
    m = jnp.max(scores, axis=-1, keepdims=True)
    p = jnp.exp(scores - m)
    s = jnp.sum(p, axis=-1)
    o = jnp.einsum('brhcij,brhcjd->brhcid', p, vv.astype(jnp.float32)) / s[..., None]
    lse = m[..., 0] + jnp.log(s)
    return _from_dilated_blocks(o, S), _from_dilated_blocks(lse, S)


def _dilated_mixture_attention(q, k, v, rel_bias):
    out, lse = None, None
    for window, dilation in DILATED_PATTERNS:
        o_g, lse_g = _dilated_window_attention(q, k, v, rel_bias, window, dilation)
        if out is None:
            out, lse = o_g, lse_g
        else:
            new_lse = jnp.logaddexp(lse, lse_g)
            out = out * jnp.exp(lse - new_lse)[..., None] + o_g * jnp.exp(lse_g - new_lse)[..., None]
            lse = new_lse
    return out


def _multiscale_pool(u, pool_w, pool_scale):
    B, S, _ = u.shape
    ug = u.astype(jnp.float32).reshape(B, S, POOL_GROUPS, POOL_GROUP_DIM)
    csum = jnp.cumsum(ug, axis=1)
    counts = jnp.arange(1, S + 1, dtype=jnp.float32)
    outs = []
    for g, w in enumerate(POOL_WINDOWS):
        cg = csum[:, :, g]
        lag = jnp.pad(cg[:, :S - w], ((0, 0), (w, 0), (0, 0)))
        mean = (cg - lag) / jnp.minimum(counts, w)[None, :, None]
        outs.append(mean - ug[:, :, g])
    y = jnp.stack(outs, axis=2)
    y = jnp.einsum('bsgc,gcd->bsgd', y, pool_w.astype(jnp.float32)).reshape(B, S, POOL_WIDTH)
    return (y * pool_scale.astype(jnp.float32)).astype(u.dtype)


def _even_layer(h, rel_bias, norm, w_in, q_gain, k_gain, pool_w, pool_scale, w_out):
    B, S, _ = h.shape
    xn = _rms_norm(h, norm)
    proj = jnp.einsum('bsd,de->bse', xn, w_in)
    q, k, v, u, z = jnp.split(proj, [ATTN_WIDTH, 2 * ATTN_WIDTH, 3 * ATTN_WIDTH,
                                     3 * ATTN_WIDTH + POOL_WIDTH], axis=-1)
    heads = lambda t: t.reshape(B, S, ATTN_HEADS, ATTN_HEAD_DIM)
    q = _rms_norm(heads(q), q_gain) * (ATTN_HEAD_DIM ** -0.5)
    k = _rms_norm(heads(k), k_gain)
    attn = _dilated_mixture_attention(q, k, heads(v), rel_bias).reshape(B, S, ATTN_WIDTH).astype(h.dtype)
    pool = _multiscale_pool(u, pool_w, pool_scale)
    mixed = jnp.concatenate([attn, pool], axis=-1) * jax.nn.silu(z)
    return jnp.einsum('bse,ed->bsd', mixed, w_out)


def _causal_depthwise_conv(x, w, b):
    K, C = w.shape
    y = lax.conv_general_dilated(x, w[:, None, :].astype(x.dtype), window_strides=(1,),
                                 padding=[(K - 1, 0)], dimension_numbers=('NWC', 'WIO', 'NWC'),
                                 feature_group_count=C)
    return y + b


def _block_diag(x, w):
    B, S, E = x.shape
    nb, bs, _ = w.shape
    return jnp.einsum('bsnc,ncd->bsnd', x.reshape(B, S, nb, bs), w).reshape(B, S, E)


def _mlstm_chunkwise(q, k, v, i_pre, f_pre):
    B, H, S, DK = q.shape
    DV = v.shape[-1]
    L = MLSTM_CHUNK
    nc = S // L
    k = k * (DK ** -0.5)
    log_f = jax.nn.log_sigmoid(f_pre)
    causal = np.tril(np.ones((L, L), dtype=bool))

    def chunks(t):
        return jnp.moveaxis(t.reshape((B, H, nc, L) + t.shape[3:]), 2, 0)

    def step(carry, inp):
        C, n, m = carry
        qc, kc, vc, ic, lfc = inp
        b = jnp.cumsum(lfc, axis=-1)
        log_d = jnp.where(causal, b[..., :, None] - b[..., None, :] + ic[..., None, :], -jnp.inf)
        log_inter = b + m[..., None]
        m_t = jnp.maximum(jnp.max(log_d, axis=-1), log_inter)
        dmat = jnp.exp(log_d - m_t[..., None])
        g = jnp.exp(log_inter - m_t)
        s = jnp.einsum('bhid,bhjd->bhij', qc, kc) * dmat
        num = jnp.einsum('bhij,bhjv->bhiv', s, vc) + g[..., None] * jnp.einsum('bhid,bhdv->bhiv', qc, C)
        den = jnp.sum(s, axis=-1) + g * jnp.einsum('bhid,bhd->bhi', qc, n)
        hc = num / jnp.maximum(jnp.abs(den), jnp.exp(-m_t))[..., None]
        m_new = m_t[..., -1]
        decay = jnp.exp(b[..., -1] + m - m_new)
        w = jnp.exp(b[..., -1:] - b + ic - m_new[..., None])
        C_new = decay[..., None, None] * C + jnp.einsum('bhj,bhjd,bhjv->bhdv', w, kc, vc)
        n_new = decay[..., None] * n + jnp.einsum('bhj,bhjd->bhd', w, kc)
        return (C_new, n_new, m_new), hc

    init = (jnp.zeros((B, H, DK, DV), jnp.float32), jnp.zeros((B, H, DK), jnp.float32),
            jnp.full((B, H), -1e30, jnp.float32))
    _, hs = lax.scan(step, init, (chunks(q), chunks(k), chunks(v), chunks(i_pre), chunks(log_f)))
    return jnp.moveaxis(hs, 0, 2).reshape(B, H, S, DV)


def _odd_layer(h, norm, w_up, conv_w, conv_b, wq, wk, wv, w_if, b_if, gn, skip, w_down):
    B, S, _ = h.shape
    f32 = jnp.float32
    xn = _rms_norm(h, norm)
    xm, z, o_pre = jnp.split(jnp.einsum('bsd,de->bse', xn, w_up), 3, axis=-1)
    xc = jax.nn.silu(_causal_depthwise_conv(xm, conv_w, conv_b))
    q = _block_diag(xc, wq)
    k = _block_diag(xc, wk)
    v = _block_diag(xm, wv)
    gates = jnp.einsum('bse,eg->bsg', jnp.concatenate([q, k, v], axis=-1), w_if) + b_if
    i_pre, f_pre = jnp.split(gates.astype(f32), 2, axis=-1)
    heads = lambda t: t.astype(f32).reshape(B, S, MLSTM_HEADS, MLSTM_HEAD_DIM).transpose(0, 2, 1, 3)
    hc = _mlstm_chunkwise(heads(q), heads(k), heads(v), i_pre.transpose(0, 2, 1), f_pre.transpose(0, 2, 1))
    mu = jnp.mean(hc, axis=-1, keepdims=True)
    var = jnp.mean(jnp.square(hc - mu), axis=-1, keepdims=True)
    hn = ((hc - mu) * lax.rsqrt(var + NORM_EPS)).transpose(0, 2, 1, 3).reshape(B, S, MLSTM_WIDTH)
    cell = jax.nn.sigmoid(o_pre.astype(f32)) * (hn * gn.astype(f32))
    out = ((cell + skip.astype(f32) * xc.astype(f32)) * jax.nn.silu(z.astype(f32))).astype(h.dtype)
    return jnp.einsum('bse,ed->bsd', out, w_down)


def setup_inputs(seed: int = 0) -> dict:
    key = jax.random.key(seed)
    ks = jax.random.split(key, 24)
    f32 = jnp.float32
    nrm = lambda kk, shape, scale: jax.random.normal(kk, shape, f32) * scale
    NE, NO, H = N_EVEN, N_ODD, MLSTM_HEADS
    b_i = nrm(ks[17], (NO, H), 0.1)
    b_f = jnp.linspace(3.0, 6.0, H, dtype=f32)[None] + nrm(ks[18], (NO, H), 0.1)
    return {
        'x': nrm(ks[0], (BATCH, SEQ, D_MODEL), 1.0),
        'rel_bias': nrm(ks[1], (REL_BUCKETS, ATTN_HEADS), 0.2),
        'e_norm': 1.0 + nrm(ks[2], (NE, D_MODEL), 0.05),
        'e_w_in': nrm(ks[3], (NE, D_MODEL, EVEN_IN_WIDTH), D_MODEL ** -0.5),
        'e_q_gain': 1.0 + nrm(ks[4], (NE, ATTN_HEAD_DIM), 0.05),
        'e_k_gain': 1.0 + nrm(ks[5], (NE, ATTN_HEAD_DIM), 0.05),
        'e_pool_w': nrm(ks[6], (NE, POOL_GROUPS, POOL_GROUP_DIM, POOL_GROUP_DIM), POOL_GROUP_DIM ** -0.5),
        'e_pool_scale': 1.0 + nrm(ks[7], (NE, POOL_WIDTH), 0.1),
        'e_w_out': nrm(ks[8], (NE, MIX_WIDTH, D_MODEL), MIX_WIDTH ** -0.5),
        'o_norm': 1.0 + nrm(ks[9], (NO, D_MODEL), 0.05),
        'o_w_up': nrm(ks[10], (NO, D_MODEL, ODD_IN_WIDTH), D_MODEL ** -0.5),
        'o_conv_w': nrm(ks[11], (NO, CONV_WIDTH, MLSTM_WIDTH), CONV_WIDTH ** -0.5),
        'o_conv_b': nrm(ks[12], (NO, MLSTM_WIDTH), 0.02),
        'o_wq': nrm(ks[13], (NO, MLSTM_WIDTH // QKV_BLOCK, QKV_BLOCK, QKV_BLOCK), QKV_BLOCK ** -0.5),
        'o_wk': nrm(ks[14], (NO, MLSTM_WIDTH // QKV_BLOCK, QKV_BLOCK, QKV_BLOCK), QKV_BLOCK ** -0.5),
        'o_wv': nrm(ks[15], (NO, MLSTM_WIDTH // QKV_BLOCK, QKV_BLOCK, QKV_BLOCK), QKV_BLOCK ** -0.5),
        'o_w_if': nrm(ks[16], (NO, 3 * MLSTM_WIDTH, 2 * H), (3 * MLSTM_WIDTH) ** -0.5),
        'o_b_if': jnp.concatenate([b_i, b_f], axis=-1),
        'o_gn': 1.0 + nrm(ks[19], (NO, MLSTM_WIDTH), 0.05),
        'o_skip': 1.0 + nrm(ks[20], (NO, MLSTM_WIDTH), 0.05),
        'o_w_down': nrm(ks[21], (NO, MLSTM_WIDTH, D_MODEL), MLSTM_WIDTH ** -0.5),
    }


def reference(x, rel_bias, e_norm, e_w_in, e_q_gain, e_k_gain, e_pool_w, e_pool_scale, e_w_out,
              o_norm, o_w_up, o_conv_w, o_conv_b, o_wq, o_wk, o_wv, o_w_if, o_b_if, o_gn, o_skip,
              o_w_down):
    h = x
    for layer in range(DEPTH):
        j = layer // 2
        if layer % 2 == 0:
            h = h + _even_layer(h, rel_bias, e_norm[j], e_w_in[j], e_q_gain[j], e_k_gain[j],
                                e_pool_w[j], e_pool_scale[j], e_w_out[j]).astype(h.dtype)
        else:
            h = h + _odd_layer(h, o_norm[j], o_w_up[j], o_conv_w[j], o_conv_b[j], o_wq[j], o_wk[j],
                               o_wv[j], o_w_if[j], o_b_if[j], o_gn[j], o_skip[j], o_w_down[j]).astype(h.dtype)
    return h
```

```python
import functools

import numpy as np
import jax
import jax.numpy as jnp
from jax import lax
from jax.experimental import pallas as pl
from jax.experimental.pallas import tpu as pltpu

F32 = jnp.float32
BF16 = jnp.bfloat16

NORM_EPS = 1e-6
LANES = 128
ATTN_HEAD_DIM = 128
ATTN_BLOCK = 128
DILATIONS = (1, 4, 16)
ATTN_CHUNK = ATTN_BLOCK * DILATIONS[-1]
POOL_WINDOWS = (2, 4, 8, 16)
POOL_HALO = 16
REL_BUCKETS = 32
REL_MAX_DIST = 2048
MLSTM_HEADS = 8
MLSTM_CHUNK = 128
CONV_WIDTH = 4
CONV_HALO = 8
QKV_BLOCK = 4
VMEM_LIMIT = 56 * 1024 * 1024


def _params(*sem):
    return pltpu.CompilerParams(dimension_semantics=sem, vmem_limit_bytes=VMEM_LIMIT)


def _silu(x):
    return x * (1.0 / (1.0 + jnp.exp(-x)))


def _rmsnorm_body(x_ref, g_ref, o_ref):
    x = x_ref[...]
    ms = jnp.mean(x * x, axis=-1, keepdims=True)
    o_ref[...] = (x * lax.rsqrt(ms + NORM_EPS) * g_ref[...]).astype(o_ref.dtype)


def _rmsnorm(h, gain, tm=256):
    S, D = h.shape
    tm = min(tm, S)
    return pl.pallas_call(
        _rmsnorm_body,
        grid=(S // tm,),
        in_specs=[pl.BlockSpec((tm, D), lambda i: (i, 0)),
                  pl.BlockSpec((1, D), lambda i: (0, 0))],
        out_specs=pl.BlockSpec((tm, D), lambda i: (i, 0)),
        out_shape=jax.ShapeDtypeStruct((S, D), BF16),
        compiler_params=_params("parallel"),
        name="rmsnorm",
    )(h, gain.reshape(1, D).astype(F32))


def _mm_body(*refs, bounds, has_res, nk):
    n_a = len(bounds)
    a_refs = refs[:n_a]
    w_ref = refs[n_a]
    res_ref = refs[n_a + 1] if has_res else None
    o_ref = refs[n_a + 1 + int(has_res)]
    if nk == 1:
        acc = jnp.dot(a_refs[0][...], w_ref[...], preferred_element_type=F32)
        if has_res:
            acc = res_ref[...] + acc
        o_ref[...] = acc.astype(o_ref.dtype)
        return
    acc_ref = refs[n_a + 2 + int(has_res)]
    k = pl.program_id(2)

    @pl.when(k == 0)
    def _():
        acc_ref[...] = res_ref[...] if has_res else jnp.zeros_like(acc_ref)

    for a_ref, (lo, hi) in zip(a_refs, bounds):
        @pl.when(jnp.logical_and(k >= lo, k < hi))
        def _(a_ref=a_ref):
            acc_ref[...] += jnp.dot(a_ref[...], w_ref[...], preferred_element_type=F32)

    @pl.when(k == nk - 1)
    def _():
        o_ref[...] = acc_ref[...].astype(o_ref.dtype)


def _matmul(a_list, w, residual=None, out_dtype=F32, tm=1024, tn=1024, tk=4096):
    M = a_list[0].shape[0]
    K, N = w.shape
    tm, tn = min(tm, M), min(tn, N)
    tk = min([tk] + [a.shape[1] for a in a_list])
    bounds, lo = [], 0
    for a in a_list:
        assert a.shape[1] % tk == 0
        bounds.append((lo, lo + a.shape[1] // tk))
        lo += a.shape[1] // tk
    nk = lo
    assert nk * tk == K and M % tm == 0 and N % tn == 0

    def a_spec(lo, hi):
        return pl.BlockSpec((tm, tk), lambda i, j, k: (i, jnp.clip(k - lo, 0, hi - lo - 1)))

    in_specs = [a_spec(lo, hi) for lo, hi in bounds]
    in_specs.append(pl.BlockSpec((tk, tn), lambda i, j, k: (k, j)))
    args = list(a_list) + [w]
    if residual is not None:
        in_specs.append(pl.BlockSpec((tm, tn), lambda i, j, k: (i, j)))
        args.append(residual)
    scratch = [pltpu.VMEM((tm, tn), F32)] if nk > 1 else []
    return pl.pallas_call(
        functools.partial(_mm_body, bounds=tuple(bounds), has_res=residual is not None, nk=nk),
        grid=(M // tm, N // tn, nk),
        in_specs=in_specs,
        out_specs=pl.BlockSpec((tm, tn), lambda i, j, k: (i, j)),
        out_shape=jax.ShapeDtypeStruct((M, N), out_dtype),
        scratch_shapes=scratch,
        compiler_params=_params("parallel", "parallel", "arbitrary"),
        name="matmul",
    )(*args)


def _t5_bucket(dist):
    max_exact = REL_BUCKETS // 2
    safe = np.maximum(dist, 1).astype(np.float32)
    large = max_exact + (np.log(safe / max_exact) / np.log(REL_MAX_DIST / max_exact)
                         * (REL_BUCKETS - max_exact)).astype(np.int32)
    large = np.minimum(large, REL_BUCKETS - 1)
    return np.where(dist < max_exact, dist, large).astype(np.int32)


def _attn_bucket_table():
    B = ATTN_BLOCK
    i = np.arange(B)[:, None]
    j = np.arange(2 * B)[None, :]
    rel = i + B - j
    band = (rel >= 0) & (rel <= B)
    tabs = []
    for first in (False, True):
        ok = band & ((j >= B) if first else True)
        for d in DILATIONS:
            tabs.append(np.where(ok, _t5_bucket(np.clip(rel, 0, None) * d), -1))
    return np.stack(tabs).astype(np.int32)


def _attn_body(relb_ref, bkt_ref, q_ref, kp_ref, kc_ref, vp_ref, vc_ref, z_ref, qg_ref, kg_ref,
               o_ref, bias_ref, qn_ref, kn_ref, vv_ref, acc_ref, m_ref, l_ref):
    B = ATTN_BLOCK
    C = ATTN_CHUNK
    h = pl.program_id(0)
    c = pl.program_id(1)

    @pl.when(c == 0)
    def _():
        for t in range(bias_ref.shape[0]):
            bkt = bkt_ref[t]
            bias = jnp.full(bkt.shape, -jnp.inf, F32)
            for b in range(REL_BUCKETS):
                bias = jnp.where(bkt == b, relb_ref[b, h], bias)
            bias_ref[t] = bias

    def _norm(x, g):
        ms = jnp.mean(x * x, axis=-1, keepdims=True)
        return x * lax.rsqrt(ms + NORM_EPS) * g

    qn_ref[...] = _norm(q_ref[...], qg_ref[...]) * (ATTN_HEAD_DIM ** -0.5)
    kn_ref[0:C, :] = _norm(kp_ref[...], kg_ref[...])
    kn_ref[C:2 * C, :] = _norm(kc_ref[...], kg_ref[...])
    vv_ref[0:C, :] = vp_ref[...]
    vv_ref[C:2 * C, :] = vc_ref[...]

    first_chunk = (c == 0).astype(jnp.int32)

    def block(q, k, v, bias):
        s = lax.dot_general(q.astype(BF16), k.astype(BF16), (((1,), (1,)), ((), ())),
                            preferred_element_type=F32) + bias
        m = jnp.max(s, axis=-1, keepdims=True)
        p = jnp.exp(s - m)
        l = jnp.sum(p, axis=-1, keepdims=True)
        o = jnp.dot(p.astype(BF16), v.astype(BF16), preferred_element_type=F32)
        return o, m, l

    def body1(b, carry):
        r0 = pl.multiple_of(b * B, B)
        q = qn_ref[pl.ds(r0, B), :]
        k = kn_ref[pl.ds(r0 + (C - B), 2 * B), :]
        v = vv_ref[pl.ds(r0 + (C - B), 2 * B), :]
        bias = bias_ref[3 * first_chunk * (b == 0).astype(jnp.int32)]
        o, m, l = block(q, k, v, bias)
        acc_ref[pl.ds(r0, B), :] = o
        m_ref[pl.ds(r0, B), :] = jnp.broadcast_to(m, (B, LANES))
        l_ref[pl.ds(r0, B), :] = jnp.broadcast_to(l, (B, LANES))
        return carry

    lax.fori_loop(0, C // B, body1, 0)

    for t, d in enumerate(DILATIONS[1:], start=1):
        span = B * d
        n_sub = C // span

        def bodyd(it, carry, t=t, d=d, span=span, n_sub=n_sub):
            sub = it // d
            r = it - sub * d
            q0 = sub * span + r
            k0 = q0 + (C - span)
            rows = pl.ds(q0, B, stride=d)
            q = qn_ref[rows, :]
            k = kn_ref[pl.ds(k0, 2 * B, stride=d), :]
            v = vv_ref[pl.ds(k0, 2 * B, stride=d), :]
            bias = bias_ref[t + 3 * first_chunk * (sub == 0).astype(jnp.int32)]
            o, m, l = block(q, k, v, bias)
            m_old = m_ref[rows, :]
            m_new = jnp.maximum(m_old, m)
            a_old = jnp.exp(m_old - m_new)
            a_new = jnp.exp(m - m_new)
            acc_ref[rows, :] = acc_ref[rows, :] * a_old + o * a_new
            l_ref[rows, :] = l_ref[rows, :] * a_old + l * a_new
            m_ref[rows, :] = m_new
            return carry

        lax.fori_loop(0, n_sub * d, bodyd, 0)

    o_ref[...] = (acc_ref[...] / l_ref[...] * _silu(z_ref[...])).astype(o_ref.dtype)


def _attention(proj, rel_bias, q_gain, k_gain, n_heads, z_col):
    S = proj.shape[0]
    C, B, Dh = ATTN_CHUNK, ATTN_BLOCK, ATTN_HEAD_DIM
    assert S % C == 0
    H = n_heads
    bkt = jnp.asarray(_attn_bucket_table())
    prev = lambda c: jnp.maximum(c - 1, 0)
    blk = lambda f: pl.BlockSpec((C, Dh), f)
    return pl.pallas_call(
        _attn_body,
        grid=(H, S // C),
        in_specs=[
            pl.BlockSpec(memory_space=pltpu.SMEM),
            pl.BlockSpec(bkt.shape, lambda h, c: (0, 0, 0)),
            blk(lambda h, c: (c, h)),
            blk(lambda h, c: (prev(c), H + h)),
            blk(lambda h, c: (c, H + h)),
            blk(lambda h, c: (prev(c), 2 * H + h)),
            blk(lambda h, c: (c, 2 * H + h)),
            blk(lambda h, c: (c, z_col + h)),
            pl.BlockSpec((1, Dh), lambda h, c: (0, 0)),
            pl.BlockSpec((1, Dh), lambda h, c: (0, 0)),
        ],
        out_specs=blk(lambda h, c: (c, h)),
        out_shape=jax.ShapeDtypeStruct((S, H * Dh), BF16),
        scratch_shapes=[
            pltpu.VMEM((6, B, 2 * B), F32),
            pltpu.VMEM((C, Dh), F32),
            pltpu.VMEM((2 * C, Dh), F32),
            pltpu.VMEM((2 * C, Dh), F32),
            pltpu.VMEM((C, Dh), F32),
            pltpu.VMEM((C, LANES), F32),
            pltpu.VMEM((C, LANES), F32),
        ],
        compiler_params=_params("arbitrary", "arbitrary"),
        name="dilated_attention",
    )(rel_bias.astype(F32), bkt, proj, proj, proj, proj, proj, proj,
      q_gain.reshape(1, Dh).astype(F32), k_gain.reshape(1, Dh).astype(F32))


def _pool_body(u_ref, halo_ref, z_ref, w_ref, sc_ref, o_ref, *, group_dim):
    T = u_ref.shape[0]
    i = pl.program_id(0)
    G = len(POOL_WINDOWS)
    halo_on = (i > 0).astype(F32)
    pos = (i * T + lax.broadcasted_iota(jnp.int32, (T, 1), 0) + 1).astype(F32)
    for g, w in enumerate(POOL_WINDOWS):
        cols = slice(g * group_dim, (g + 1) * group_dim)
        x = u_ref[:, cols]
        e = jnp.concatenate([halo_ref[:, cols] * halo_on, x], axis=0)
        width = 1
        while width < w:
            e = e[width:, :] + e[:-width, :]
            width *= 2
        off = POOL_HALO - (w - 1)
        win = e[off:off + T, :]
        y = win / jnp.minimum(pos, float(w)) - x
        yp = jnp.dot(y.astype(BF16), w_ref[g].astype(BF16), preferred_element_type=F32)
        o_ref[:, cols] = (yp * sc_ref[:, cols] * _silu(z_ref[:, cols])).astype(o_ref.dtype)


def _pool(proj, pool_w, pool_scale, u_col, z_col, tile=256):
    S = proj.shape[0]
    G, Cg, _ = pool_w.shape
    P = G * Cg
    T = min(tile, S)
    assert u_col % P == 0 and z_col % P == 0 and T % POOL_HALO == 0
    return pl.pallas_call(
        functools.partial(_pool_body, group_dim=Cg),
        grid=(S // T,),
        in_specs=[
            pl.BlockSpec((T, P), lambda i: (i, u_col // P)),
            pl.BlockSpec((POOL_HALO, P), lambda i: (jnp.maximum(i * (T // POOL_HALO) - 1, 0), u_col // P)),
            pl.BlockSpec((T, P), lambda i: (i, z_col // P)),
            pl.BlockSpec((G, Cg, Cg), lambda i: (0, 0, 0)),
            pl.BlockSpec((1, P), lambda i: (0, 0)),
        ],
        out_specs=pl.BlockSpec((T, P), lambda i: (i, 0)),
        out_shape=jax.ShapeDtypeStruct((S, P), BF16),
        compiler_params=_params("parallel"),
        name="multiscale_pool",
    )(proj, proj, proj, pool_w.astype(F32), pool_scale.reshape(1, P).astype(F32))


def _front_body(xm_ref, halo_ref, cw_ref, cb_ref, wq_ref, wk_ref, wv_ref, wif_ref, bif_ref,
                xc_ref, q_ref, k_ref, v_ref, g_ref):
    T, TC = xm_ref.shape
    i = pl.program_id(0)
    j = pl.program_id(1)
    xm = xm_ref[...]
    halo = halo_ref[...] * (i > 0).astype(F32)
    e = jnp.concatenate([halo, xm], axis=0)
    conv = cb_ref[...]
    for t in range(CONV_WIDTH):
        off = CONV_HALO - (CONV_WIDTH - 1) + t
        conv = conv + e[off:off + T, :] * cw_ref[t:t + 1, :]
    xc = _silu(conv)
    xc_ref[...] = xc
    xcb = xc.astype(BF16)
    xmb = xm.astype(BF16)
    gates = jnp.zeros(g_ref.shape, F32)
    for (src, w_ref, o_ref, p) in ((xcb, wq_ref, q_ref, 0), (xcb, wk_ref, k_ref, 1), (xmb, wv_ref, v_ref, 2)):
        for n in range(TC // LANES):
            cols = slice(n * LANES, (n + 1) * LANES)
            y = jnp.dot(src[:, cols], w_ref[n], preferred_element_type=F32)
            yb = y.astype(BF16)
            o_ref[:, cols] = yb
            gates = gates + jnp.dot(yb, wif_ref[p, cols, :], preferred_element_type=F32)

    @pl.when(j == 0)
    def _():
        g_ref[...] = bif_ref[...] + gates

    @pl.when(j > 0)
    def _():
        g_ref[...] += gates


def _block_diag_dense(w):
    nb, bs, _ = w.shape
    per = LANES // bs
    wg = w.reshape(nb // per, per, bs, bs)
    eye = jnp.eye(per, dtype=w.dtype)
    dense = jnp.einsum('gpcd,pq->gpcqd', wg, eye)
    return dense.reshape(nb // per, LANES, LANES).astype(BF16)


def _mlstm_front(up, conv_w, conv_b, wq, wk, wv, w_if, b_if, tile=512, tcol=1024):
    S = up.shape[0]
    E = conv_w.shape[1]
    T, TC = min(tile, S), min(tcol, E)
    NG = w_if.shape[1]
    nt = TC // LANES
    wdense = [_block_diag_dense(w) for w in (wq, wk, wv)]
    wif = w_if.reshape(3, E, NG).astype(BF16)
    row = lambda: pl.BlockSpec((T, TC), lambda i, j: (i, j))
    return pl.pallas_call(
        _front_body,
        grid=(S // T, E // TC),
        in_specs=[
            row(),
            pl.BlockSpec((CONV_HALO, TC), lambda i, j: (jnp.maximum(i * (T // CONV_HALO) - 1, 0), j)),
            pl.BlockSpec((CONV_WIDTH, TC), lambda i, j: (0, j)),
            pl.BlockSpec((1, TC), lambda i, j: (0, j)),
            pl.BlockSpec((nt, LANES, LANES), lambda i, j: (j, 0, 0)),
            pl.BlockSpec((nt, LANES, LANES), lambda i, j: (j, 0, 0)),
            pl.BlockSpec((nt, LANES, LANES), lambda i, j: (j, 0, 0)),
            pl.BlockSpec((3, TC, NG), lambda i, j: (0, j, 0)),
            pl.BlockSpec((1, NG), lambda i, j: (0, 0)),
        ],
        out_specs=[row(), row(), row(), row(), pl.BlockSpec((T, NG), lambda i, j: (i, 0))],
        out_shape=[jax.ShapeDtypeStruct((S, E), F32)] + [jax.ShapeDtypeStruct((S, E), BF16)] * 3
        + [jax.ShapeDtypeStruct((S, NG), F32)],
        compiler_params=_params("parallel", "arbitrary"),
        name="mlstm_front",
    )(up, up, conv_w.astype(F32), conv_b.reshape(1, E).astype(F32), *wdense, wif,
      b_if.reshape(1, NG).astype(F32))


def _mlstm_body(q_ref, k_ref, v_ref, ig_ref, fg_ref, op_ref, xc_ref, z_ref, gn_ref, sk_ref,
                o_ref, c_ref, cb_ref, n_ref, m_ref):
    L, DK = q_ref.shape
    c = pl.program_id(1)

    @pl.when(c == 0)
    def _():
        c_ref[...] = jnp.zeros_like(c_ref)
        cb_ref[...] = jnp.zeros_like(cb_ref)
        n_ref[...] = jnp.zeros_like(n_ref)
        m_ref[...] = jnp.full(m_ref.shape, -1e30, F32)

    ri = lax.broadcasted_iota(jnp.int32, (L, L), 0)
    cj = lax.broadcasted_iota(jnp.int32, (L, L), 1)

    i_row = ig_ref[0]
    f_row = fg_ref[0]
    lf_row = jnp.minimum(f_row, 0.0) - jnp.log1p(jnp.exp(-jnp.abs(f_row)))
    b8 = jnp.broadcast_to(lf_row, (8, L))
    lane8 = lax.broadcasted_iota(jnp.int32, (8, L), 1)
    sh = 1
    while sh < L:
        b8 = b8 + jnp.where(lane8 >= sh, pltpu.roll(b8, sh, 1), 0.0)
        sh *= 2
    b_row = b8[0:1, :]
    stacked = jnp.where(ri == 0, jnp.broadcast_to(b_row, (L, L)),
                        jnp.where(ri == 1, jnp.broadcast_to(i_row, (L, L)), 0.0))
    stacked_t = stacked.T
    b_col = stacked_t[:, 0:1]
    i_col = stacked_t[:, 1:2]

    m_prev = m_ref[0:1, 0:1]
    log_d = jnp.where(cj <= ri, b_col - b_row + i_row, -jnp.inf)
    log_inter = b_col + m_prev
    m_t = jnp.maximum(jnp.max(log_d, axis=-1, keepdims=True), log_inter)
    scale = DK ** -0.5
    dmat = jnp.exp(log_d - m_t) * scale
    g = jnp.exp(log_inter - m_t)

    q = q_ref[...]
    k = k_ref[...]
    v = v_ref[...]
    s = lax.dot_general(q, k, (((1,), (1,)), ((), ())), preferred_element_type=F32) * dmat
    inter = jnp.dot(q, cb_ref[...], preferred_element_type=F32)
    num = jnp.dot(s.astype(BF16), v, preferred_element_type=F32) + g * inter
    qn = jnp.sum(q.astype(F32) * n_ref[...], axis=-1, keepdims=True)
    den = jnp.sum(s, axis=-1, keepdims=True) + g * qn
    hc = num / jnp.maximum(jnp.abs(den), jnp.exp(-m_t))

    m_new = m_t[L - 1:L, :]
    b_last = b_col[L - 1:L, :]
    decay = jnp.exp(b_last + m_prev - m_new)
    w_col = jnp.exp(b_last - b_col + i_col - m_new) * scale
    vw = (v.astype(F32) * w_col).astype(BF16)
    upd = lax.dot_general(k, vw, (((0,), (0,)), ((), ())), preferred_element_type=F32)
    c_new = c_ref[...] * decay + upd
    c_ref[...] = c_new
    cb_ref[...] = c_new.astype(BF16)
    n_ref[...] = n_ref[...] * decay + jnp.sum(k.astype(F32) * w_col, axis=0, keepdims=True)
    m_ref[...] = jnp.broadcast_to(m_new, m_ref.shape)

    mu = jnp.mean(hc, axis=-1, keepdims=True)
    ctr = hc - mu
    var = jnp.mean(ctr * ctr, axis=-1, keepdims=True)
    hn = ctr * lax.rsqrt(var + NORM_EPS)
    cell = (1.0 / (1.0 + jnp.exp(-op_ref[...]))) * (hn * gn_ref[...])
    o_ref[...] = ((cell + sk_ref[...] * xc_ref[...]) * _silu(z_ref[...])).astype(o_ref.dtype)


def _mlstm(q, k, v, gates, up, xc, gn, skip):
    S, E = q.shape
    H, L = MLSTM_HEADS, MLSTM_CHUNK
    DH = E // H
    NCH = S // L
    gt = gates.T.reshape(2 * H, NCH, 1, L)
    blk = lambda col0: pl.BlockSpec((L, DH), lambda h, c: (c, col0 + h))
    vec = pl.BlockSpec((1, DH), lambda h, c: (0, h))
    return pl.pallas_call(
        _mlstm_body,
        grid=(H, NCH),
        in_specs=[
            blk(0), blk(0), blk(0),
            pl.BlockSpec((None, 1, 1, L), lambda h, c: (h, c, 0, 0)),
            pl.BlockSpec((None, 1, 1, L), lambda h, c: (H + h, c, 0, 0)),
            blk(2 * H), blk(0), blk(H),
            vec, vec,
        ],
        out_specs=blk(0),
        out_shape=jax.ShapeDtypeStruct((S, E), BF16),
        scratch_shapes=[
            pltpu.VMEM((DH, DH), F32),
            pltpu.VMEM((DH, DH), BF16),
            pltpu.VMEM((1, DH), F32),
            pltpu.VMEM((8, LANES), F32),
        ],
        compiler_params=_params("arbitrary", "arbitrary"),
        name="mlstm_chunkwise",
    )(q, k, v, gt, gt, up, xc, up, gn.reshape(1, E).astype(F32), skip.reshape(1, E).astype(F32))


def _even_layer(h, rel_bias, norm, w_in, q_gain, k_gain, pool_w, pool_scale, w_out):
    D = h.shape[1]
    mix = w_out.shape[0]
    pool_width = pool_w.shape[0] * pool_w.shape[1]
    attn_width = mix - pool_width
    n_heads = attn_width // ATTN_HEAD_DIM
    u_col = 3 * attn_width
    z_col = u_col + pool_width
    xn = _rmsnorm(h, norm)
    proj = _matmul([xn], w_in.astype(BF16))
    attn = _attention(proj, rel_bias, q_gain, k_gain, n_heads, z_col // ATTN_HEAD_DIM)
    pool = _pool(proj, pool_w, pool_scale, u_col, z_col + attn_width)
    return _matmul([attn, pool], w_out.astype(BF16), residual=h, tk=pool_width)


def _odd_layer(h, norm, w_up, conv_w, conv_b, wq, wk, wv, w_if, b_if, gn, skip, w_down):
    xn = _rmsnorm(h, norm)
    up = _matmul([xn], w_up.astype(BF16))
    xc, q, k, v, gates = _mlstm_front(up, conv_w, conv_b, wq, wk, wv, w_if, b_if)
    out = _mlstm(q, k, v, gates, up, xc, gn, skip)
    return _matmul([out], w_down.astype(BF16), residual=h, tk=2048)


def kernel(x, rel_bias, e_norm, e_w_in, e_q_gain, e_k_gain, e_pool_w, e_pool_scale, e_w_out,
           o_norm, o_w_up, o_conv_w, o_conv_b, o_wq, o_wk, o_wv, o_w_if, o_b_if, o_gn, o_skip,
           o_w_down):
    B, S, D = x.shape
    depth = e_norm.shape[0] + o_norm.shape[0]
    outs = []
    for b in range(B):
        h = x[b]
        for layer in range(depth):
            j = layer // 2
            if layer % 2 == 0:
                h = _even_layer(h, rel_bias, e_norm[j], e_w_in[j], e_q_gain[j], e_k_gain[j],
                                e_pool_w[j], e_pool_scale[j], e_w_out[j])
            else:
                h = _odd_layer(h, o_norm[j], o_w_up[j], o_conv_w[j], o_conv_b[j], o_wq[j], o_wk[j],
                               o_wv[j], o_w_if[j], o_b_if[j], o_gn[j], o_skip[j], o_w_down[j])
        outs.append(h)
    return jnp.stack(outs)
```

```python
import functools

import numpy as np
import jax
import jax.numpy as jnp
from jax import lax
from jax.experimental import pallas as pl
from jax.experimental.pallas import tpu as pltpu

F32 = jnp.float32
BF16 = jnp.bfloat16

NORM_EPS = 1e-6
LANES = 128
ATTN_HEAD_DIM = 128
ATTN_BLOCK = 128
DILATIONS = (1, 4, 16)
ATTN_CHUNK = ATTN_BLOCK * DILATIONS[-1]
ATTN_UNROLL = 4
POOL_WINDOWS = (2, 4, 8, 16)
POOL_HALO = 16
REL_BUCKETS = 32
REL_MAX_DIST = 2048
MLSTM_HEADS = 8
MLSTM_CHUNK = 128
CONV_WIDTH = 4
CONV_HALO = 8
QKV_BLOCK = 4
VMEM_LIMIT = 56 * 1024 * 1024


def _params(*sem):
    return pltpu.CompilerParams(dimension_semantics=sem, vmem_limit_bytes=VMEM_LIMIT)


def _silu(x):
    return x * (1.0 / (1.0 + jnp.exp(-x)))


def _rmsnorm_body(x_ref, g_ref, o_ref):
    x = x_ref[...]
    ms = jnp.mean(x * x, axis=-1, keepdims=True)
    o_ref[...] = (x * lax.rsqrt(ms + NORM_EPS) * g_ref[...]).astype(o_ref.dtype)


def _rmsnorm(h, gain, tm=256):
    S, D = h.shape
    tm = min(tm, S)
    return pl.pallas_call(
        _rmsnorm_body,
        grid=(S // tm,),
        in_specs=[pl.BlockSpec((tm, D), lambda i: (i, 0)),
                  pl.BlockSpec((1, D), lambda i: (0, 0))],
        out_specs=pl.BlockSpec((tm, D), lambda i: (i, 0)),
        out_shape=jax.ShapeDtypeStruct((S, D), BF16),
        compiler_params=_params("parallel"),
        name="rmsnorm",
    )(h, gain.reshape(1, D).astype(F32))


def _mm_body(*refs, bounds, has_res, nk):
    n_a = len(bounds)
    a_refs = refs[:n_a]
    w_ref = refs[n_a]
    res_ref = refs[n_a + 1] if has_res else None
    o_ref = refs[n_a + 1 + int(has_res)]
    if nk == 1:
        acc = jnp.dot(a_refs[0][...], w_ref[...], preferred_element_type=F32)
        if has_res:
            acc = res_ref[...] + acc
        o_ref[...] = acc.astype(o_ref.dtype)
        return
    acc_ref = refs[n_a + 2 + int(has_res)]
    k = pl.program_id(2)

    @pl.when(k == 0)
    def _():
        acc_ref[...] = res_ref[...] if has_res else jnp.zeros_like(acc_ref)

    for a_ref, (lo, hi) in zip(a_refs, bounds):
        @pl.when(jnp.logical_and(k >= lo, k < hi))
        def _(a_ref=a_ref):
            acc_ref[...] += jnp.dot(a_ref[...], w_ref[...], preferred_element_type=F32)

    @pl.when(k == nk - 1)
    def _():
        o_ref[...] = acc_ref[...].astype(o_ref.dtype)


def _matmul(a_list, w, residual=None, out_dtype=F32, tm=1024, tn=1024, tk=4096):
    M = a_list[0].shape[0]
    K, N = w.shape
    tm, tn = min(tm, M), min(tn, N)
    tk = min([tk] + [a.shape[1] for a in a_list])
    bounds, lo = [], 0
    for a in a_list:
        assert a.shape[1] % tk == 0
        bounds.append((lo, lo + a.shape[1] // tk))
        lo += a.shape[1] // tk
    nk = lo
    assert nk * tk == K and M % tm == 0 and N % tn == 0

    def a_spec(lo, hi):
        return pl.BlockSpec((tm, tk), lambda i, j, k: (i, jnp.clip(k - lo, 0, hi - lo - 1)))

    in_specs = [a_spec(lo, hi) for lo, hi in bounds]
    in_specs.append(pl.BlockSpec((tk, tn), lambda i, j, k: (k, j)))
    args = list(a_list) + [w]
    if residual is not None:
        in_specs.append(pl.BlockSpec((tm, tn), lambda i, j, k: (i, j)))
        args.append(residual)
    scratch = [pltpu.VMEM((tm, tn), F32)] if nk > 1 else []
    return pl.pallas_call(
        functools.partial(_mm_body, bounds=tuple(bounds), has_res=residual is not None, nk=nk),
        grid=(M // tm, N // tn, nk),
        in_specs=in_specs,
        out_specs=pl.BlockSpec((tm, tn), lambda i, j, k: (i, j)),
        out_shape=jax.ShapeDtypeStruct((M, N), out_dtype),
        scratch_shapes=scratch,
        compiler_params=_params("parallel", "parallel", "arbitrary"),
        name="matmul",
    )(*args)


def _t5_bucket(dist):
    max_exact = REL_BUCKETS // 2
    safe = np.maximum(dist, 1).astype(np.float32)
    large = max_exact + (np.log(safe / max_exact) / np.log(REL_MAX_DIST / max_exact)
                         * (REL_BUCKETS - max_exact)).astype(np.int32)
    large = np.minimum(large, REL_BUCKETS - 1)
    return np.where(dist < max_exact, dist, large).astype(np.int32)


def _attn_bucket_table():
    B = ATTN_BLOCK
    i = np.arange(B)[:, None]
    j = np.arange(2 * B)[None, :]
    rel = i + B - j
    band = (rel >= 0) & (rel <= B)
    tabs = [np.where(band, _t5_bucket(np.clip(rel, 0, None) * d), -1) for d in DILATIONS]
    return np.stack(tabs).astype(np.int32)


def _attn_body(relb_ref, bkt_ref, q_ref, k_ref, v_ref, z_ref, qg_ref, kg_ref,
               o_ref, bias_ref, qn_ref, q4_ref, kn_ref, k4_ref, vn_ref, v4_ref,
               o1_ref, m1_ref, l1_ref, o2_ref, m2_ref, l2_ref, o3_ref, m3_ref, l3_ref, out_ref):
    B = ATTN_BLOCK
    C = ATTN_CHUNK
    R4 = DILATIONS[1]
    Q = C // R4
    h = pl.program_id(0)
    c = pl.program_id(1)
    n_dil = len(DILATIONS)
    cur = c % 2
    prv = 1 - cur

    @pl.when(c == 0)
    def _():
        col = lax.broadcasted_iota(jnp.int32, (B, 2 * B), 1)
        for t in range(n_dil):
            bkt = bkt_ref[t]
            bias = jnp.full(bkt.shape, -jnp.inf, F32)
            for b in range(REL_BUCKETS):
                bias = jnp.where(bkt == b, relb_ref[b, h], bias)
            bias_ref[t] = bias
            bias_ref[t + n_dil] = jnp.where(col >= B, bias, -jnp.inf)
        kn_ref[1] = jnp.zeros(kn_ref.shape[1:], F32)
        vn_ref[1] = jnp.zeros(vn_ref.shape[1:], F32)
        k4_ref[1] = jnp.zeros(k4_ref.shape[1:], F32)
        v4_ref[1] = jnp.zeros(v4_ref.shape[1:], F32)

    def _norm(x, g):
        ms = jnp.mean(x * x, axis=-1, keepdims=True)
        return x * lax.rsqrt(ms + NORM_EPS) * g

    qn_ref[...] = _norm(q_ref[...], qg_ref[...]) * (ATTN_HEAD_DIM ** -0.5)
    kn_ref[cur] = _norm(k_ref[...], kg_ref[...])
    vn_ref[cur] = v_ref[...]
    for r4 in range(R4):
        rows = pl.ds(r4, Q, stride=R4)
        q4_ref[r4] = qn_ref[rows, :]
        k4_ref[cur, r4] = kn_ref[cur, rows, :]
        v4_ref[cur, r4] = vn_ref[cur, rows, :]

    first_chunk = (c == 0).astype(jnp.int32)

    def block(q, k, v, bias):
        s = lax.dot_general(q.astype(BF16), k.astype(BF16), (((1,), (1,)), ((), ())),
                            preferred_element_type=F32) + bias
        m = jnp.max(s, axis=-1, keepdims=True)
        p = jnp.exp(s - m)
        l = jnp.sum(p, axis=-1, keepdims=True)
        o = jnp.dot(p.astype(BF16), v.astype(BF16), preferred_element_type=F32)
        return o, m, l

    def bcast(x):
        return jnp.broadcast_to(x, (B, LANES))

    def body1(g, carry):
        for u in range(ATTN_UNROLL):
            b = g * ATTN_UNROLL + u
            r0 = pl.multiple_of(b * B, B)
            is0 = (b == 0).astype(jnp.int32)
            slot_a = cur + is0 * (prv - cur)
            ra = pl.multiple_of(r0 - B + is0 * C, B)
            q = qn_ref[pl.ds(r0, B), :]
            k = jnp.concatenate([kn_ref[slot_a, pl.ds(ra, B), :], kn_ref[cur, pl.ds(r0, B), :]], axis=0)
            v = jnp.concatenate([vn_ref[slot_a, pl.ds(ra, B), :], vn_ref[cur, pl.ds(r0, B), :]], axis=0)
            o, m, l = block(q, k, v, bias_ref[n_dil * first_chunk * is0])
            o1_ref[pl.ds(r0, B), :] = o
            m1_ref[pl.ds(r0, B), :] = bcast(m)
            l1_ref[pl.ds(r0, B), :] = bcast(l)
        return carry

    lax.fori_loop(0, C // B // ATTN_UNROLL, body1, 0, unroll=True)

    def body2(sub, carry):
        is0 = (sub == 0).astype(jnp.int32)
        slot_a = cur + is0 * (prv - cur)
        r0 = pl.multiple_of(sub * B, B)
        ra = pl.multiple_of(r0 - B + is0 * Q, B)
        bias = bias_ref[1 + n_dil * first_chunk * is0]
        for r4 in range(R4):
            q = q4_ref[r4, pl.ds(r0, B), :]
            k = jnp.concatenate([k4_ref[slot_a, r4, pl.ds(ra, B), :], k4_ref[cur, r4, pl.ds(r0, B), :]], axis=0)
            v = jnp.concatenate([v4_ref[slot_a, r4, pl.ds(ra, B), :], v4_ref[cur, r4, pl.ds(r0, B), :]], axis=0)
            o, m, l = block(q, k, v, bias)
            o2_ref[r4, pl.ds(r0, B), :] = o
            m2_ref[r4, pl.ds(r0, B), :] = bcast(m)
            l2_ref[r4, pl.ds(r0, B), :] = bcast(l)
        return carry

    lax.fori_loop(0, Q // B, body2, 0, unroll=True)

    def body3(o4, carry):
        bias = bias_ref[2 + n_dil * first_chunk]
        rows = pl.ds(o4, B, stride=R4)
        for r4 in range(R4):
            q = q4_ref[r4, rows, :]
            k = jnp.concatenate([k4_ref[prv, r4, rows, :], k4_ref[cur, r4, rows, :]], axis=0)
            v = jnp.concatenate([v4_ref[prv, r4, rows, :], v4_ref[cur, r4, rows, :]], axis=0)
            o, m, l = block(q, k, v, bias)
            o3_ref[r4, rows, :] = o
            m3_ref[r4, rows, :] = bcast(m)
            l3_ref[r4, rows, :] = bcast(l)
        return carry

    lax.fori_loop(0, Q // B, body3, 0, unroll=True)

    def body_merge(sub, carry):
        r0 = pl.multiple_of(sub * B, B)
        for r4 in range(R4):
            nat = pl.ds(sub * (B * R4) + r4, B, stride=R4)
            m1, m2, m3 = m1_ref[nat, :], m2_ref[r4, pl.ds(r0, B), :], m3_ref[r4, pl.ds(r0, B), :]
            mx = jnp.maximum(jnp.maximum(m1, m2), m3)
            w1, w2, w3 = jnp.exp(m1 - mx), jnp.exp(m2 - mx), jnp.exp(m3 - mx)
            num = (o1_ref[nat, :] * w1 + o2_ref[r4, pl.ds(r0, B), :] * w2
                   + o3_ref[r4, pl.ds(r0, B), :] * w3)
            den = (l1_ref[nat, :] * w1 + l2_ref[r4, pl.ds(r0, B), :] * w2
                   + l3_ref[r4, pl.ds(r0, B), :] * w3)
            out_ref[nat, :] = num / den
        return carry

    lax.fori_loop(0, Q // B, body_merge, 0)

    o_ref[...] = (out_ref[...] * _silu(z_ref[...])).astype(o_ref.dtype)


def _attention(proj, rel_bias, q_gain, k_gain, n_heads, z_col):
    S = proj.shape[0]
    C, B, Dh = ATTN_CHUNK, ATTN_BLOCK, ATTN_HEAD_DIM
    assert S % C == 0
    H = n_heads
    bkt = jnp.asarray(_attn_bucket_table())
    R4 = DILATIONS[1]
    blk = lambda f: pl.BlockSpec((C, Dh), f)
    nat = pltpu.VMEM((C, Dh), F32)
    mod4 = pltpu.VMEM((R4, C // R4, Dh), F32)
    return pl.pallas_call(
        _attn_body,
        grid=(H, S // C),
        in_specs=[
            pl.BlockSpec(memory_space=pltpu.SMEM),
            pl.BlockSpec(bkt.shape, lambda h, c: (0, 0, 0)),
            blk(lambda h, c: (c, h)),
            blk(lambda h, c: (c, H + h)),
            blk(lambda h, c: (c, 2 * H + h)),
            blk(lambda h, c: (c, z_col + h)),
            pl.BlockSpec((1, Dh), lambda h, c: (0, 0)),
            pl.BlockSpec((1, Dh), lambda h, c: (0, 0)),
        ],
        out_specs=blk(lambda h, c: (c, h)),
        out_shape=jax.ShapeDtypeStruct((S, H * Dh), BF16),
        scratch_shapes=[
            pltpu.VMEM((2 * len(DILATIONS), B, 2 * B), F32),
            nat, mod4,
            pltpu.VMEM((2, C, Dh), F32), pltpu.VMEM((2, R4, C // R4, Dh), F32),
            pltpu.VMEM((2, C, Dh), F32), pltpu.VMEM((2, R4, C // R4, Dh), F32),
            nat, nat, nat,
            mod4, mod4, mod4,
            mod4, mod4, mod4,
            nat,
        ],
        compiler_params=_params("arbitrary", "arbitrary"),
        name="dilated_attention",
    )(rel_bias.astype(F32), bkt, proj, proj, proj, proj,
      q_gain.reshape(1, Dh).astype(F32), k_gain.reshape(1, Dh).astype(F32))


def _pool_body(u_ref, halo_ref, z_ref, w_ref, sc_ref, o_ref, *, group_dim):
    T = u_ref.shape[0]
    i = pl.program_id(0)
    G = len(POOL_WINDOWS)
    halo_on = (i > 0).astype(F32)
    pos = (i * T + lax.broadcasted_iota(jnp.int32, (T, 1), 0) + 1).astype(F32)
    for g, w in enumerate(POOL_WINDOWS):
        cols = slice(g * group_dim, (g + 1) * group_dim)
        x = u_ref[:, cols]
        e = jnp.concatenate([halo_ref[:, cols] * halo_on, x], axis=0)
        width = 1
        while width < w:
            e = e[width:, :] + e[:-width, :]
            width *= 2
        off = POOL_HALO - (w - 1)
        win = e[off:off + T, :]
        y = win / jnp.minimum(pos, float(w)) - x
        yp = jnp.dot(y.astype(BF16), w_ref[g].astype(BF16), preferred_element_type=F32)
        o_ref[:, cols] = (yp * sc_ref[:, cols] * _silu(z_ref[:, cols])).astype(o_ref.dtype)


def _pool(proj, pool_w, pool_scale, u_col, z_col, tile=256):
    S = proj.shape[0]
    G, Cg, _ = pool_w.shape
    P = G * Cg
    T = min(tile, S)
    assert u_col % P == 0 and z_col % P == 0 and T % POOL_HALO == 0
    return pl.pallas_call(
        functools.partial(_pool_body, group_dim=Cg),
        grid=(S // T,),
        in_specs=[
            pl.BlockSpec((T, P), lambda i: (i, u_col // P)),
            pl.BlockSpec((POOL_HALO, P), lambda i: (jnp.maximum(i * (T // POOL_HALO) - 1, 0), u_col // P)),
            pl.BlockSpec((T, P), lambda i: (i, z_col // P)),
            pl.BlockSpec((G, Cg, Cg), lambda i: (0, 0, 0)),
            pl.BlockSpec((1, P), lambda i: (0, 0)),
        ],
        out_specs=pl.BlockSpec((T, P), lambda i: (i, 0)),
        out_shape=jax.ShapeDtypeStruct((S, P), BF16),
        compiler_params=_params("parallel"),
        name="multiscale_pool",
    )(proj, proj, proj, pool_w.astype(F32), pool_scale.reshape(1, P).astype(F32))


def _front_body(xm_ref, halo_ref, cw_ref, cb_ref, wq_ref, wk_ref, wv_ref, wif_ref, bif_ref,
                xc_ref, q_ref, k_ref, v_ref, g_ref):
    T, TC = xm_ref.shape
    i = pl.program_id(0)
    j = pl.program_id(1)
    xm = xm_ref[...]
    halo = halo_ref[...] * (i > 0).astype(F32)
    e = jnp.concatenate([halo, xm], axis=0)
    conv = cb_ref[...]
    for t in range(CONV_WIDTH):
        off = CONV_HALO - (CONV_WIDTH - 1) + t
        conv = conv + e[off:off + T, :] * cw_ref[t:t + 1, :]
    xc = _silu(conv)
    xc_ref[...] = xc
    xcb = xc.astype(BF16)
    xmb = xm.astype(BF16)
    gates = jnp.zeros(g_ref.shape, F32)
    for (src, w_ref, o_ref, p) in ((xcb, wq_ref, q_ref, 0), (xcb, wk_ref, k_ref, 1), (xmb, wv_ref, v_ref, 2)):
        for n in range(TC // LANES):
            cols = slice(n * LANES, (n + 1) * LANES)
            y = jnp.dot(src[:, cols], w_ref[n], preferred_element_type=F32)
            yb = y.astype(BF16)
            o_ref[:, cols] = yb
            gates = gates + jnp.dot(yb, wif_ref[p, cols, :], preferred_element_type=F32)

    @pl.when(j == 0)
    def _():
        g_ref[...] = bif_ref[...] + gates

    @pl.when(j > 0)
    def _():
        g_ref[...] += gates


def _block_diag_dense(w):
    nb, bs, _ = w.shape
    per = LANES // bs
    wg = w.reshape(nb // per, per, bs, bs)
    eye = jnp.eye(per, dtype=w.dtype)
    dense = jnp.einsum('gpcd,pq->gpcqd', wg, eye)
    return dense.reshape(nb // per, LANES, LANES).astype(BF16)


def _mlstm_front(up, conv_w, conv_b, wq, wk, wv, w_if, b_if, tile=512, tcol=1024):
    S = up.shape[0]
    E = conv_w.shape[1]
    T, TC = min(tile, S), min(tcol, E)
    NG = w_if.shape[1]
    nt = TC // LANES
    wdense = [_block_diag_dense(w) for w in (wq, wk, wv)]
    wif = w_if.reshape(3, E, NG).astype(BF16)
    row = lambda: pl.BlockSpec((T, TC), lambda i, j: (i, j))
    return pl.pallas_call(
        _front_body,
        grid=(S // T, E // TC),
        in_specs=[
            row(),
            pl.BlockSpec((CONV_HALO, TC), lambda i, j: (jnp.maximum(i * (T // CONV_HALO) - 1, 0), j)),
            pl.BlockSpec((CONV_WIDTH, TC), lambda i, j: (0, j)),
            pl.BlockSpec((1, TC), lambda i, j: (0, j)),
            pl.BlockSpec((nt, LANES, LANES), lambda i, j: (j, 0, 0)),
            pl.BlockSpec((nt, LANES, LANES), lambda i, j: (j, 0, 0)),
            pl.BlockSpec((nt, LANES, LANES), lambda i, j: (j, 0, 0)),
            pl.BlockSpec((3, TC, NG), lambda i, j: (0, j, 0)),
            pl.BlockSpec((1, NG), lambda i, j: (0, 0)),
        ],
        out_specs=[row(), row(), row(), row(), pl.BlockSpec((T, NG), lambda i, j: (i, 0))],
        out_shape=[jax.ShapeDtypeStruct((S, E), F32)] + [jax.ShapeDtypeStruct((S, E), BF16)] * 3
        + [jax.ShapeDtypeStruct((S, NG), F32)],
        compiler_params=_params("parallel", "arbitrary"),
        name="mlstm_front",
    )(up, up, conv_w.astype(F32), conv_b.reshape(1, E).astype(F32), *wdense, wif,
      b_if.reshape(1, NG).astype(F32))


def _mlstm_body(q_ref, k_ref, v_ref, ig_ref, fg_ref, op_ref, xc_ref, z_ref, gn_ref, sk_ref,
                o_ref, c_ref, cb_ref, n_ref, m_ref):
    L, DK = q_ref.shape
    c = pl.program_id(1)

    @pl.when(c == 0)
    def _():
        c_ref[...] = jnp.zeros_like(c_ref)
        cb_ref[...] = jnp.zeros_like(cb_ref)
        n_ref[...] = jnp.zeros_like(n_ref)
        m_ref[...] = jnp.full(m_ref.shape, -1e30, F32)

    ri = lax.broadcasted_iota(jnp.int32, (L, L), 0)
    cj = lax.broadcasted_iota(jnp.int32, (L, L), 1)

    i_row = ig_ref[0]
    f_row = fg_ref[0]
    lf_row = jnp.minimum(f_row, 0.0) - jnp.log1p(jnp.exp(-jnp.abs(f_row)))
    b8 = jnp.broadcast_to(lf_row, (8, L))
    lane8 = lax.broadcasted_iota(jnp.int32, (8, L), 1)
    sh = 1
    while sh < L:
        b8 = b8 + jnp.where(lane8 >= sh, pltpu.roll(b8, sh, 1), 0.0)
        sh *= 2
    b_row = b8[0:1, :]
    stacked = jnp.where(ri == 0, jnp.broadcast_to(b_row, (L, L)),
                        jnp.where(ri == 1, jnp.broadcast_to(i_row, (L, L)), 0.0))
    stacked_t = stacked.T
    b_col = stacked_t[:, 0:1]
    i_col = stacked_t[:, 1:2]

    m_prev = m_ref[0:1, 0:1]
    log_d = jnp.where(cj <= ri, b_col - b_row + i_row, -jnp.inf)
    log_inter = b_col + m_prev
    m_t = jnp.maximum(jnp.max(log_d, axis=-1, keepdims=True), log_inter)
    scale = DK ** -0.5
    dmat = jnp.exp(log_d - m_t) * scale
    g = jnp.exp(log_inter - m_t)

    q = q_ref[...]
    k = k_ref[...]
    v = v_ref[...]
    s = lax.dot_general(q, k, (((1,), (1,)), ((), ())), preferred_element_type=F32) * dmat
    inter = jnp.dot(q, cb_ref[...], preferred_element_type=F32)
    num = jnp.dot(s.astype(BF16), v, preferred_element_type=F32) + g * inter
    qn = jnp.sum(q.astype(F32) * n_ref[...], axis=-1, keepdims=True)
    den = jnp.sum(s, axis=-1, keepdims=True) + g * qn
    hc = num / jnp.maximum(jnp.abs(den), jnp.exp(-m_t))

    m_new = m_t[L - 1:L, :]
    b_last = b_col[L - 1:L, :]
    decay = jnp.exp(b_last + m_prev - m_new)
    w_col = jnp.exp(b_last - b_col + i_col - m_new) * scale
    vw = (v.astype(F32) * w_col).astype(BF16)
    upd = lax.dot_general(k, vw, (((0,), (0,)), ((), ())), preferred_element_type=F32)
    c_new = c_ref[...] * decay + upd
    c_ref[...] = c_new
    cb_ref[...] = c_new.astype(BF16)
    n_ref[...] = n_ref[...] * decay + jnp.sum(k.astype(F32) * w_col, axis=0, keepdims=True)
    m_ref[...] = jnp.broadcast_to(m_new, m_ref.shape)

    mu = jnp.mean(hc, axis=-1, keepdims=True)
    ctr = hc - mu
    var = jnp.mean(ctr * ctr, axis=-1, keepdims=True)
    hn = ctr * lax.rsqrt(var + NORM_EPS)
    cell = (1.0 / (1.0 + jnp.exp(-op_ref[...]))) * (hn * gn_ref[...])
    o_ref[...] = ((cell + sk_ref[...] * xc_ref[...]) * _silu(z_ref[...])).astype(o_ref.dtype)


def _mlstm(q, k, v, gates, up, xc, gn, skip):
    S, E = q.shape
    H, L = MLSTM_HEADS, MLSTM_CHUNK
    DH = E // H
    NCH = S // L
    gt = gates.T.reshape(2 * H, NCH, 1, L)
    blk = lambda col0: pl.BlockSpec((L, DH), lambda h, c: (c, col0 + h))
    vec = pl.BlockSpec((1, DH), lambda h, c: (0, h))
    return pl.pallas_call(
        _mlstm_body,
        grid=(H, NCH),
        in_specs=[
            blk(0), blk(0), blk(0),
            pl.BlockSpec((None, 1, 1, L), lambda h, c: (h, c, 0, 0)),
            pl.BlockSpec((None, 1, 1, L), lambda h, c: (H + h, c, 0, 0)),
            blk(2 * H), blk(0), blk(H),
            vec, vec,
        ],
        out_specs=blk(0),
        out_shape=jax.ShapeDtypeStruct((S, E), BF16),
        scratch_shapes=[
            pltpu.VMEM((DH, DH), F32),
            pltpu.VMEM((DH, DH), BF16),
            pltpu.VMEM((1, DH), F32),
            pltpu.VMEM((8, LANES), F32),
        ],
        compiler_params=_params("arbitrary", "arbitrary"),
        name="mlstm_chunkwise",
    )(q, k, v, gt, gt, up, xc, up, gn.reshape(1, E).astype(F32), skip.reshape(1, E).astype(F32))


def _even_layer(h, rel_bias, norm, w_in, q_gain, k_gain, pool_w, pool_scale, w_out):
    D = h.shape[1]
    mix = w_out.shape[0]
    pool_width = pool_w.shape[0] * pool_w.shape[1]
    attn_width = mix - pool_width
    n_heads = attn_width // ATTN_HEAD_DIM
    u_col = 3 * attn_width
    z_col = u_col + pool_width
    xn = _rmsnorm(h, norm)
    proj = _matmul([xn], w_in.astype(BF16))
    attn = _attention(proj, rel_bias, q_gain, k_gain, n_heads, z_col // ATTN_HEAD_DIM)
    pool = _pool(proj, pool_w, pool_scale, u_col, z_col + attn_width)
    return _matmul([attn, pool], w_out.astype(BF16), residual=h, tk=pool_width)


def _odd_layer(h, norm, w_up, conv_w, conv_b, wq, wk, wv, w_if, b_if, gn, skip, w_down):
    xn = _rmsnorm(h, norm)
    up = _matmul([xn], w_up.astype(BF16))
    xc, q, k, v, gates = _mlstm_front(up, conv_w, conv_b, wq, wk, wv, w_if, b_if)
    out = _mlstm(q, k, v, gates, up, xc, gn, skip)
    return _matmul([out], w_down.astype(BF16), residual=h, tk=2048)


def kernel(x, rel_bias, e_norm, e_w_in, e_q_gain, e_k_gain, e_pool_w, e_pool_scale, e_w_out,
           o_norm, o_w_up, o_conv_w, o_conv_b, o_wq, o_wk, o_wv, o_w_if, o_b_if, o_gn, o_skip,
           o_w_down):
    B, S, D = x.shape
    depth = e_norm.shape[0] + o_norm.shape[0]
    outs = []
    for b in range(B):
        h = x[b]
        for layer in range(depth):
            j = layer // 2
            if layer % 2 == 0:
                h = _even_layer(h, rel_bias, e_norm[j], e_w_in[j], e_q_gain[j], e_k_gain[j],
                                e_pool_w[j], e_pool_scale[j], e_w_out[j])
            else:
                h = _odd_layer(h, o_norm[j], o_w_up[j], o_conv_w[j], o_conv_b[j], o_wq[j], o_wk[j],
                               o_wv[j], o_w_if[j], o_b_if[j], o_gn[j], o_skip[j], o_w_down[j])
        outs.append(h)
    return jnp.stack(outs)
```

```python
import functools

import numpy as np
import jax
import jax.numpy as jnp
from jax import lax
from jax.experimental import pallas as pl
from jax.experimental.pallas import tpu as pltpu

F32 = jnp.float32
BF16 = jnp.bfloat16

NORM_EPS = 1e-6
LANES = 128
ATTN_HEAD_DIM = 128
ATTN_BLOCK = 128
DILATIONS = (1, 4, 16)
ATTN_CHUNK = ATTN_BLOCK * DILATIONS[-1]
POOL_WINDOWS = (2, 4, 8, 16)
POOL_HALO = 16
REL_BUCKETS = 32
REL_MAX_DIST = 2048
MLSTM_HEADS = 8
MLSTM_CHUNK = 256
MLSTM_GROUP = 2
CONV_WIDTH = 4
CONV_HALO = 8
QKV_TILE = 128
VMEM_LIMIT = 56 * 1024 * 1024


def _params(*sem):
    return pltpu.CompilerParams(dimension_semantics=sem, vmem_limit_bytes=VMEM_LIMIT)


def _silu(x):
    return x * (1.0 / (1.0 + jnp.exp(-x)))


def _rmsnorm_body(x_ref, g_ref, o_ref):
    x = x_ref[...]
    ms = jnp.mean(x * x, axis=-1, keepdims=True)
    o_ref[...] = (x * lax.rsqrt(ms + NORM_EPS) * g_ref[...]).astype(o_ref.dtype)


def _rmsnorm(h, gain, tm=256):
    S, D = h.shape
    tm = min(tm, S)
    return pl.pallas_call(
        _rmsnorm_body,
        grid=(S // tm,),
        in_specs=[pl.BlockSpec((tm, D), lambda i: (i, 0)),
                  pl.BlockSpec((1, D), lambda i: (0, 0))],
        out_specs=pl.BlockSpec((tm, D), lambda i: (i, 0)),
        out_shape=jax.ShapeDtypeStruct((S, D), BF16),
        compiler_params=_params("parallel"),
        name="rmsnorm",
    )(h, gain.reshape(1, D).astype(F32))


def _mm_body(*refs, n_a, has_res):
    a_refs = refs[:n_a]
    w_refs = refs[n_a:2 * n_a]
    res_ref = refs[2 * n_a] if has_res else None
    o_ref = refs[2 * n_a + int(has_res)]
    acc = res_ref[...] if has_res else None
    for a_ref, w_ref in zip(a_refs, w_refs):
        d = jnp.dot(a_ref[...], w_ref[...].astype(BF16), preferred_element_type=F32)
        acc = d if acc is None else acc + d
    o_ref[...] = acc.astype(o_ref.dtype)


def _matmul(a_list, w_stack, layer, residual=None, out_dtype=F32, tm=1024, tn=512):
    M = a_list[0].shape[0]
    _, K, N = w_stack.shape
    tm, tn = min(tm, M), min(tn, N)
    assert M % tm == 0 and N % tn == 0 and sum(a.shape[1] for a in a_list) == K
    in_specs, w_specs, row0 = [], [], 0
    for a in a_list:
        kp = a.shape[1]
        assert row0 % kp == 0
        in_specs.append(pl.BlockSpec((tm, kp), lambda i, j: (i, 0), pipeline_mode=pl.Buffered(1)))
        w_specs.append(pl.BlockSpec((None, kp, tn), lambda i, j, rb=row0 // kp: (layer, rb, j)))
        row0 += kp
    in_specs += w_specs
    args = list(a_list) + [w_stack] * len(a_list)
    if residual is not None:
        in_specs.append(pl.BlockSpec((tm, tn), lambda i, j: (i, j)))
        args.append(residual)
    return pl.pallas_call(
        functools.partial(_mm_body, n_a=len(a_list), has_res=residual is not None),
        grid=(M // tm, N // tn),
        in_specs=in_specs,
        out_specs=pl.BlockSpec((tm, tn), lambda i, j: (i, j)),
        out_shape=jax.ShapeDtypeStruct((M, N), out_dtype),
        compiler_params=_params("parallel", "arbitrary"),
        name="matmul",
    )(*args)


def _t5_bucket(dist):
    max_exact = REL_BUCKETS // 2
    safe = np.maximum(dist, 1).astype(np.float32)
    large = max_exact + (np.log(safe / max_exact) / np.log(REL_MAX_DIST / max_exact)
                         * (REL_BUCKETS - max_exact)).astype(np.int32)
    large = np.minimum(large, REL_BUCKETS - 1)
    return np.where(dist < max_exact, dist, large).astype(np.int32)


def _attn_bucket_table():
    B = ATTN_BLOCK
    i = np.arange(B)[:, None]
    j = np.arange(2 * B)[None, :]
    rel = i + B - j
    band = (rel >= 0) & (rel <= B)
    tabs = [np.where(band, _t5_bucket(np.clip(rel, 0, None) * d), -1) for d in DILATIONS]
    return np.stack(tabs).astype(np.int32)


def _attn_body(relb_ref, bkt_ref, q_ref, k_ref, v_ref, z_ref, qg_ref, kg_ref,
               o_ref, bias_ref, qn_ref, q4_ref, kn_ref, k4_ref, vn_ref, v4_ref,
               o1_ref, m1_ref, l1_ref, o2_ref, m2_ref, l2_ref, o3_ref, m3_ref, l3_ref, out_ref):
    B = ATTN_BLOCK
    C = ATTN_CHUNK
    R4 = DILATIONS[1]
    Q = C // R4
    h = pl.program_id(0)
    c = pl.program_id(1)
    n_dil = len(DILATIONS)
    cur = c % 2
    prv = 1 - cur

    @pl.when(c == 0)
    def _():
        col = lax.broadcasted_iota(jnp.int32, (B, 2 * B), 1)
        for t in range(n_dil):
            bkt = bkt_ref[t]
            bias = jnp.full(bkt.shape, -jnp.inf, F32)
            for b in range(REL_BUCKETS):
                bias = jnp.where(bkt == b, relb_ref[b, h], bias)
            bias_ref[t] = bias
            bias_ref[t + n_dil] = jnp.where(col >= B, bias, -jnp.inf)
        kn_ref[1] = jnp.zeros(kn_ref.shape[1:], F32)
        vn_ref[1] = jnp.zeros(vn_ref.shape[1:], F32)
        k4_ref[1] = jnp.zeros(k4_ref.shape[1:], F32)
        v4_ref[1] = jnp.zeros(v4_ref.shape[1:], F32)

    def _norm(x, g):
        ms = jnp.mean(x * x, axis=-1, keepdims=True)
        return x * lax.rsqrt(ms + NORM_EPS) * g

    qn_ref[...] = _norm(q_ref[...], qg_ref[...]) * (ATTN_HEAD_DIM ** -0.5)
    kn_ref[cur] = _norm(k_ref[...], kg_ref[...])
    vn_ref[cur] = v_ref[...]
    for r4 in range(R4):
        rows = pl.ds(r4, Q, stride=R4)
        q4_ref[r4] = qn_ref[rows, :]
        k4_ref[cur, r4] = kn_ref[cur, rows, :]
        v4_ref[cur, r4] = vn_ref[cur, rows, :]

    first_chunk = jnp.where(c == 0, 1, 0)

    def block(q, k, v, bias):
        s = lax.dot_general(q.astype(BF16), k.astype(BF16), (((1,), (1,)), ((), ())),
                            preferred_element_type=F32) + bias
        m = jnp.max(s, axis=-1, keepdims=True)
        p = jnp.exp(s - m)
        l = jnp.sum(p, axis=-1, keepdims=True)
        o = jnp.dot(p.astype(BF16), v.astype(BF16), preferred_element_type=F32)
        return o, m, l

    def bcast(x):
        return jnp.broadcast_to(x, (B, LANES))

    for b in range(C // B):
        rb = slice(b * B, (b + 1) * B)
        if b == 0:
            ka, va = kn_ref[prv, C - B:C, :], vn_ref[prv, C - B:C, :]
            bias = bias_ref[n_dil * first_chunk]
        else:
            ka, va = kn_ref[cur, (b - 1) * B:b * B, :], vn_ref[cur, (b - 1) * B:b * B, :]
            bias = bias_ref[0]
        k = jnp.concatenate([ka, kn_ref[cur, rb, :]], axis=0)
        v = jnp.concatenate([va, vn_ref[cur, rb, :]], axis=0)
        o, m, l = block(qn_ref[rb, :], k, v, bias)
        o1_ref[rb, :] = o
        m1_ref[rb, :] = bcast(m)
        l1_ref[rb, :] = bcast(l)

    for sub in range(Q // B):
        rb = slice(sub * B, (sub + 1) * B)
        bias = bias_ref[1 + n_dil * first_chunk] if sub == 0 else bias_ref[1]
        for r4 in range(R4):
            if sub == 0:
                ka, va = k4_ref[prv, r4, Q - B:Q, :], v4_ref[prv, r4, Q - B:Q, :]
            else:
                ra = slice((sub - 1) * B, sub * B)
                ka, va = k4_ref[cur, r4, ra, :], v4_ref[cur, r4, ra, :]
            k = jnp.concatenate([ka, k4_ref[cur, r4, rb, :]], axis=0)
            v = jnp.concatenate([va, v4_ref[cur, r4, rb, :]], axis=0)
            o, m, l = block(q4_ref[r4, rb, :], k, v, bias)
            o2_ref[r4, rb, :] = o
            m2_ref[r4, rb, :] = bcast(m)
            l2_ref[r4, rb, :] = bcast(l)

    bias = bias_ref[2 + n_dil * first_chunk]
    for o4 in range(Q // B):
        rows = pl.ds(o4, B, stride=R4)
        for r4 in range(R4):
            k = jnp.concatenate([k4_ref[prv, r4, rows, :], k4_ref[cur, r4, rows, :]], axis=0)
            v = jnp.concatenate([v4_ref[prv, r4, rows, :], v4_ref[cur, r4, rows, :]], axis=0)
            o, m, l = block(q4_ref[r4, rows, :], k, v, bias)
            o3_ref[r4, rows, :] = o
            m3_ref[r4, rows, :] = bcast(m)
            l3_ref[r4, rows, :] = bcast(l)

    def body_merge(sub, carry):
        r0 = pl.multiple_of(sub * B, B)
        for r4 in range(R4):
            nat = pl.ds(sub * (B * R4) + r4, B, stride=R4)
            m1, m2, m3 = m1_ref[nat, :], m2_ref[r4, pl.ds(r0, B), :], m3_ref[r4, pl.ds(r0, B), :]
            mx = jnp.maximum(jnp.maximum(m1, m2), m3)
            w1, w2, w3 = jnp.exp(m1 - mx), jnp.exp(m2 - mx), jnp.exp(m3 - mx)
            num = (o1_ref[nat, :] * w1 + o2_ref[r4, pl.ds(r0, B), :] * w2
                   + o3_ref[r4, pl.ds(r0, B), :] * w3)
            den = (l1_ref[nat, :] * w1 + l2_ref[r4, pl.ds(r0, B), :] * w2
                   + l3_ref[r4, pl.ds(r0, B), :] * w3)
            out_ref[nat, :] = num / den
        return carry

    lax.fori_loop(0, Q // B, body_merge, 0)

    o_ref[...] = (out_ref[...] * _silu(z_ref[...])).astype(o_ref.dtype)


def _attention(proj, rel_bias, q_gain, k_gain, n_heads, z_col):
    S = proj.shape[0]
    C, B, Dh = ATTN_CHUNK, ATTN_BLOCK, ATTN_HEAD_DIM
    assert S % C == 0
    H = n_heads
    bkt = jnp.asarray(_attn_bucket_table())
    R4 = DILATIONS[1]
    blk = lambda f: pl.BlockSpec((C, Dh), f)
    nat = pltpu.VMEM((C, Dh), F32)
    mod4 = pltpu.VMEM((R4, C // R4, Dh), F32)
    return pl.pallas_call(
        _attn_body,
        grid=(H, S // C),
        in_specs=[
            pl.BlockSpec(memory_space=pltpu.SMEM),
            pl.BlockSpec(bkt.shape, lambda h, c: (0, 0, 0)),
            blk(lambda h, c: (c, h)),
            blk(lambda h, c: (c, H + h)),
            blk(lambda h, c: (c, 2 * H + h)),
            blk(lambda h, c: (c, z_col + h)),
            pl.BlockSpec((1, Dh), lambda h, c: (0, 0)),
            pl.BlockSpec((1, Dh), lambda h, c: (0, 0)),
        ],
        out_specs=blk(lambda h, c: (c, h)),
        out_shape=jax.ShapeDtypeStruct((S, H * Dh), BF16),
        scratch_shapes=[
            pltpu.VMEM((2 * len(DILATIONS), B, 2 * B), F32),
            nat, mod4,
            pltpu.VMEM((2, C, Dh), F32), pltpu.VMEM((2, R4, C // R4, Dh), F32),
            pltpu.VMEM((2, C, Dh), F32), pltpu.VMEM((2, R4, C // R4, Dh), F32),
            nat, nat, nat,
            mod4, mod4, mod4,
            mod4, mod4, mod4,
            nat,
        ],
        compiler_params=_params("arbitrary", "arbitrary"),
        name="dilated_attention",
    )(rel_bias.astype(F32), bkt, proj, proj, proj, proj,
      q_gain.reshape(1, Dh).astype(F32), k_gain.reshape(1, Dh).astype(F32))


def _pool_body(u_ref, halo_ref, z_ref, w_ref, sc_ref, o_ref, *, group_dim):
    T = u_ref.shape[0]
    i = pl.program_id(0)
    G = len(POOL_WINDOWS)
    halo_on = jnp.where(i > 0, 1.0, 0.0)
    pos = (i * T + lax.broadcasted_iota(jnp.int32, (T, 1), 0) + 1).astype(F32)
    for g, w in enumerate(POOL_WINDOWS):
        cols = slice(g * group_dim, (g + 1) * group_dim)
        x = u_ref[:, cols]
        e = jnp.concatenate([halo_ref[:, cols] * halo_on, x], axis=0)
        width = 1
        while width < w:
            e = e[width:, :] + e[:-width, :]
            width *= 2
        off = POOL_HALO - (w - 1)
        win = e[off:off + T, :]
        y = win / jnp.minimum(pos, float(w)) - x
        yp = jnp.dot(y.astype(BF16), w_ref[g].astype(BF16), preferred_element_type=F32)
        o_ref[:, cols] = (yp * sc_ref[:, cols] * _silu(z_ref[:, cols])).astype(o_ref.dtype)


def _pool(proj, pool_w, pool_scale, u_col, z_col, tile=256):
    S = proj.shape[0]
    G, Cg, _ = pool_w.shape
    P = G * Cg
    T = min(tile, S)
    assert u_col % P == 0 and z_col % P == 0 and T % POOL_HALO == 0
    return pl.pallas_call(
        functools.partial(_pool_body, group_dim=Cg),
        grid=(S // T,),
        in_specs=[
            pl.BlockSpec((T, P), lambda i: (i, u_col // P)),
            pl.BlockSpec((POOL_HALO, P), lambda i: (jnp.maximum(i * (T // POOL_HALO) - 1, 0), u_col // P)),
            pl.BlockSpec((T, P), lambda i: (i, z_col // P)),
            pl.BlockSpec((G, Cg, Cg), lambda i: (0, 0, 0)),
            pl.BlockSpec((1, P), lambda i: (0, 0)),
        ],
        out_specs=pl.BlockSpec((T, P), lambda i: (i, 0)),
        out_shape=jax.ShapeDtypeStruct((S, P), BF16),
        compiler_params=_params("parallel"),
        name="multiscale_pool",
    )(proj, proj, proj, pool_w.astype(F32), pool_scale.reshape(1, P).astype(F32))


def _front_body(xm_ref, halo_ref, cw_ref, cb_ref, wq_ref, wk_ref, wv_ref, wif_ref, bif_ref,
                xc_ref, q_ref, k_ref, v_ref, g_ref):
    T, TC = xm_ref.shape
    i = pl.program_id(0)
    j = pl.program_id(1)
    xm = xm_ref[...]
    halo = halo_ref[...] * jnp.where(i > 0, 1.0, 0.0)
    e = jnp.concatenate([halo, xm], axis=0)
    conv = cb_ref[...]
    for t in range(CONV_WIDTH):
        off = CONV_HALO - (CONV_WIDTH - 1) + t
        conv = conv + e[off:off + T, :] * cw_ref[t:t + 1, :]
    xc = _silu(conv)
    xc_ref[...] = xc
    xcb = xc.astype(BF16)
    xmb = xm.astype(BF16)
    gates = jnp.zeros(g_ref.shape, F32)
    W = wq_ref.shape[-1]
    for (src, w_ref, o_ref, p) in ((xcb, wq_ref, q_ref, 0), (xcb, wk_ref, k_ref, 1), (xmb, wv_ref, v_ref, 2)):
        for n in range(TC // W):
            cols = slice(n * W, (n + 1) * W)
            y = jnp.dot(src[:, cols], w_ref[n], preferred_element_type=F32)
            yb = y.astype(BF16)
            o_ref[:, cols] = yb
            gates = gates + jnp.dot(yb, wif_ref[p, cols, :], preferred_element_type=F32)

    @pl.when(j == 0)
    def _():
        g_ref[...] = bif_ref[...] + gates

    @pl.when(j > 0)
    def _():
        g_ref[...] += gates


def _block_diag_dense(w, width):
    nb, bs, _ = w.shape
    per = width // bs
    wg = w.reshape(nb // per, per, bs, bs)
    eye = jnp.eye(per, dtype=w.dtype)
    dense = jnp.einsum('gpcd,pq->gpcqd', wg, eye)
    return dense.reshape(nb // per, width, width).astype(BF16)


def _mlstm_front(up, conv_w, conv_b, wq, wk, wv, w_if, b_if, tile=512, tcol=1024):
    S = up.shape[0]
    E = conv_w.shape[1]
    T, TC = min(tile, S), min(tcol, E)
    NG = w_if.shape[1]
    W = min(QKV_TILE, TC)
    nt = TC // W
    wdense = [_block_diag_dense(w, W) for w in (wq, wk, wv)]
    wif = w_if.reshape(3, E, NG).astype(BF16)
    row = lambda: pl.BlockSpec((T, TC), lambda i, j: (i, j))
    return pl.pallas_call(
        _front_body,
        grid=(S // T, E // TC),
        in_specs=[
            row(),
            pl.BlockSpec((CONV_HALO, TC), lambda i, j: (jnp.maximum(i * (T // CONV_HALO) - 1, 0), j)),
            pl.BlockSpec((CONV_WIDTH, TC), lambda i, j: (0, j)),
            pl.BlockSpec((1, TC), lambda i, j: (0, j)),
            pl.BlockSpec((nt, W, W), lambda i, j: (j, 0, 0)),
            pl.BlockSpec((nt, W, W), lambda i, j: (j, 0, 0)),
            pl.BlockSpec((nt, W, W), lambda i, j: (j, 0, 0)),
            pl.BlockSpec((3, TC, NG), lambda i, j: (0, j, 0)),
            pl.BlockSpec((1, NG), lambda i, j: (0, 0)),
        ],
        out_specs=[row(), row(), row(), row(), pl.BlockSpec((T, NG), lambda i, j: (i, 0))],
        out_shape=[jax.ShapeDtypeStruct((S, E), F32)] + [jax.ShapeDtypeStruct((S, E), BF16)] * 3
        + [jax.ShapeDtypeStruct((S, NG), F32)],
        compiler_params=_params("parallel", "arbitrary"),
        name="mlstm_front",
    )(up, up, conv_w.astype(F32), conv_b.reshape(1, E).astype(F32), *wdense, wif,
      b_if.reshape(1, NG).astype(F32))


def _mlstm_body(q_ref, k_ref, v_ref, ig_ref, fg_ref, op_ref, xc_ref, z_ref, gn_ref, sk_ref,
                o_ref, c_ref, cb_ref, n_ref, m_ref):
    L = q_ref.shape[0]
    G, DK, _ = c_ref.shape
    c = pl.program_id(1)

    @pl.when(c == 0)
    def _():
        c_ref[...] = jnp.zeros_like(c_ref)
        cb_ref[...] = jnp.zeros_like(cb_ref)
        n_ref[...] = jnp.zeros_like(n_ref)
        m_ref[...] = jnp.full(m_ref.shape, -1e30, F32)

    ri = lax.broadcasted_iota(jnp.int32, (L, L), 0)
    cj = lax.broadcasted_iota(jnp.int32, (L, L), 1)
    lane8 = lax.broadcasted_iota(jnp.int32, (8, L), 1)
    scale = DK ** -0.5

    for hh in range(G):
        cols = slice(hh * DK, (hh + 1) * DK)
        i_row = ig_ref[hh, 0]
        f_row = fg_ref[hh, 0]
        lf_row = jnp.minimum(f_row, 0.0) - jnp.log1p(jnp.exp(-jnp.abs(f_row)))
        b8 = jnp.broadcast_to(lf_row, (8, L))
        sh = 1
        while sh < L:
            b8 = b8 + jnp.where(lane8 >= sh, pltpu.roll(b8, sh, 1), 0.0)
            sh *= 2
        b_row = b8[0:1, :]
        stacked = jnp.where(ri == 0, jnp.broadcast_to(b_row, (L, L)),
                            jnp.where(ri == 1, jnp.broadcast_to(i_row, (L, L)), 0.0))
        stacked_t = stacked.T
        b_col = stacked_t[:, 0:1]
        i_col = stacked_t[:, 1:2]

        m_prev = m_ref[hh, 0:1, 0:1]
        log_d = jnp.where(cj <= ri, b_col - b_row + i_row, -jnp.inf)
        log_inter = b_col + m_prev
        m_t = jnp.maximum(jnp.max(log_d, axis=-1, keepdims=True), log_inter)
        dmat = jnp.exp(log_d - m_t) * scale
        g = jnp.exp(log_inter - m_t)

        q = q_ref[:, cols]
        k = k_ref[:, cols]
        v = v_ref[:, cols]
        s = lax.dot_general(q, k, (((1,), (1,)), ((), ())), preferred_element_type=F32) * dmat
        inter = jnp.dot(q, cb_ref[hh], preferred_element_type=F32)
        num = jnp.dot(s.astype(BF16), v, preferred_element_type=F32) + g * inter
        qn = jnp.sum(q.astype(F32) * n_ref[hh], axis=-1, keepdims=True)
        den = jnp.sum(s, axis=-1, keepdims=True) + g * qn
        hc = num / jnp.maximum(jnp.abs(den), jnp.exp(-m_t))

        m_new = m_t[L - 1:L, :]
        b_last = b_col[L - 1:L, :]
        decay = jnp.exp(b_last + m_prev - m_new)
        w_col = jnp.exp(b_last - b_col + i_col - m_new) * scale
        vw = (v.astype(F32) * w_col).astype(BF16)
        upd = lax.dot_general(k, vw, (((0,), (0,)), ((), ())), preferred_element_type=F32)
        c_new = upd + c_ref[hh] * decay
        c_ref[hh] = c_new
        cb_ref[hh] = c_new.astype(BF16)
        n_ref[hh] = n_ref[hh] * decay + jnp.sum(k.astype(F32) * w_col, axis=0, keepdims=True)
        m_ref[hh] = jnp.broadcast_to(m_new, m_ref.shape[1:])

        mu = jnp.mean(hc, axis=-1, keepdims=True)
        ctr = hc - mu
        var = jnp.mean(ctr * ctr, axis=-1, keepdims=True)
        hn = ctr * lax.rsqrt(var + NORM_EPS)
        cell = (1.0 / (1.0 + jnp.exp(-op_ref[:, cols]))) * (hn * gn_ref[:, cols])
        o_ref[:, cols] = ((cell + sk_ref[:, cols] * xc_ref[:, cols])
                          * _silu(z_ref[:, cols])).astype(o_ref.dtype)


def _mlstm(q, k, v, gates, up, xc, gn, skip):
    S, E = q.shape
    H, L, G = MLSTM_HEADS, MLSTM_CHUNK, MLSTM_GROUP
    DH = E // H
    NCH = S // L
    HG = H // G
    gt = gates.T.reshape(2 * H, NCH, 1, L)
    blk = lambda col0: pl.BlockSpec((L, G * DH), lambda h, c: (c, col0 + h))
    vec = pl.BlockSpec((1, G * DH), lambda h, c: (0, h))
    return pl.pallas_call(
        _mlstm_body,
        grid=(HG, NCH),
        in_specs=[
            blk(0), blk(0), blk(0),
            pl.BlockSpec((G, 1, 1, L), lambda h, c: (h, c, 0, 0)),
            pl.BlockSpec((G, 1, 1, L), lambda h, c: (HG + h, c, 0, 0)),
            blk(2 * HG), blk(0), blk(HG),
            vec, vec,
        ],
        out_specs=blk(0),
        out_shape=jax.ShapeDtypeStruct((S, E), BF16),
        scratch_shapes=[
            pltpu.VMEM((G, DH, DH), F32),
            pltpu.VMEM((G, DH, DH), BF16),
            pltpu.VMEM((G, 1, DH), F32),
            pltpu.VMEM((G, 8, LANES), F32),
        ],
        compiler_params=_params("arbitrary", "arbitrary"),
        name="mlstm_chunkwise",
    )(q, k, v, gt, gt, up, xc, up, gn.reshape(1, E).astype(F32), skip.reshape(1, E).astype(F32))


def _even_layer(h, j, rel_bias, norm, w_in, q_gain, k_gain, pool_w, pool_scale, w_out):
    mix = w_out.shape[1]
    pool_width = pool_w.shape[0] * pool_w.shape[1]
    attn_width = mix - pool_width
    n_heads = attn_width // ATTN_HEAD_DIM
    u_col = 3 * attn_width
    z_col = u_col + pool_width
    xn = _rmsnorm(h, norm)
    proj = _matmul([xn], w_in, j)
    attn = _attention(proj, rel_bias, q_gain, k_gain, n_heads, z_col // ATTN_HEAD_DIM)
    pool = _pool(proj, pool_w, pool_scale, u_col, z_col + attn_width)
    return _matmul([attn, pool], w_out, j, residual=h, tn=256)


def _odd_layer(h, j, norm, w_up, conv_w, conv_b, wq, wk, wv, w_if, b_if, gn, skip, w_down):
    xn = _rmsnorm(h, norm)
    up = _matmul([xn], w_up, j)
    xc, q, k, v, gates = _mlstm_front(up, conv_w, conv_b, wq, wk, wv, w_if, b_if)
    out = _mlstm(q, k, v, gates, up, xc, gn, skip)
    return _matmul([out], w_down, j, residual=h, tn=256)


def kernel(x, rel_bias, e_norm, e_w_in, e_q_gain, e_k_gain, e_pool_w, e_pool_scale, e_w_out,
           o_norm, o_w_up, o_conv_w, o_conv_b, o_wq, o_wk, o_wv, o_w_if, o_b_if, o_gn, o_skip,
           o_w_down):
    B, S, D = x.shape
    depth = e_norm.shape[0] + o_norm.shape[0]
    outs = []
    for b in range(B):
        h = x[b]
        for layer in range(depth):
            j = layer // 2
            if layer % 2 == 0:
                h = _even_layer(h, j, rel_bias, e_norm[j], e_w_in, e_q_gain[j], e_k_gain[j],
                                e_pool_w[j], e_pool_scale[j], e_w_out)
            else:
                h = _odd_layer(h, j, o_norm[j], o_w_up, o_conv_w[j], o_conv_b[j], o_wq[j], o_wk[j],
                               o_wv[j], o_w_if[j], o_b_if[j], o_gn[j], o_skip[j], o_w_down)
        outs.append(h)
    return jnp.stack(outs)
```

```python
import functools

import numpy as np
import jax
import jax.numpy as jnp
from jax import lax
from jax.experimental import pallas as pl
from jax.experimental.pallas import tpu as pltpu

F32 = jnp.float32
BF16 = jnp.bfloat16

NORM_EPS = 1e-6
IN_PROJ_ROWS = 2048
OUT_PROJ_COLS = 256
LOG2E = 1.4426950408889634
LANES = 128
ATTN_HEAD_DIM = 128
ATTN_BLOCK = 128
DILATIONS = (1, 4, 16)
ATTN_CHUNK = ATTN_BLOCK * DILATIONS[-1]
POOL_WINDOWS = (2, 4, 8, 16)
POOL_HALO = 16
REL_BUCKETS = 32
REL_MAX_DIST = 2048
MLSTM_HEADS = 8
MLSTM_CHUNK = 256
MLSTM_GROUP = 2
CONV_WIDTH = 4
CONV_HALO = 8
QKV_TILE = 128
VMEM_LIMIT = 56 * 1024 * 1024


def _params(*sem):
    return pltpu.CompilerParams(dimension_semantics=sem, vmem_limit_bytes=VMEM_LIMIT)


def _silu(x):
    return x * (1.0 / (1.0 + jnp.exp(-x)))


def _mm_body(*refs, n_a, has_res, has_scale, has_next, n_cols):
    a_refs = refs[:n_a]
    w_refs = refs[n_a:2 * n_a]
    p = 2 * n_a
    res_ref = refs[p] if has_res else None
    p += int(has_res)
    scale_ref = refs[p] if has_scale else None
    p += int(has_scale)
    gain_ref = refs[p] if has_next else None
    p += int(has_next)
    o_ref = refs[p]
    acc = None
    for a_ref, w_ref in zip(a_refs, w_refs):
        d = jnp.dot(a_ref[...], w_ref[...].astype(BF16), preferred_element_type=F32)
        acc = d if acc is None else acc + d
    if has_scale:
        acc = acc * scale_ref[:, 0:1]
    if has_res:
        acc = res_ref[...] + acc
    o_ref[...] = acc.astype(o_ref.dtype)
    if has_next:
        xg_ref, r_ref, ssq_ref = refs[p + 1], refs[p + 2], refs[p + 3]
        j = pl.program_id(1)
        xg_ref[...] = (acc * gain_ref[...]).astype(BF16)
        part = jnp.broadcast_to(jnp.sum(acc * acc, axis=-1, keepdims=True), ssq_ref.shape)

        @pl.when(j == 0)
        def _():
            ssq_ref[...] = part

        @pl.when(j > 0)
        def _():
            ssq_ref[...] += part

        @pl.when(j == pl.num_programs(1) - 1)
        def _():
            r_ref[...] = lax.rsqrt(ssq_ref[...] * (1.0 / n_cols) + NORM_EPS)


def _matmul(a_list, w_stack, layer, residual=None, row_scale=None, next_gain=None,
            out_dtype=F32, tm=1024, tn=512):
    M = a_list[0].shape[0]
    _, K, N = w_stack.shape
    tm, tn = min(tm, M), min(tn, N)
    assert M % tm == 0 and N % tn == 0 and sum(a.shape[1] for a in a_list) == K
    in_specs, w_specs, row0 = [], [], 0
    for a in a_list:
        kp = a.shape[1]
        assert row0 % kp == 0
        in_specs.append(pl.BlockSpec((tm, kp), lambda i, j: (i, 0), pipeline_mode=pl.Buffered(1)))
        w_specs.append(pl.BlockSpec((None, kp, tn), lambda i, j, rb=row0 // kp: (layer, rb, j)))
        row0 += kp
    in_specs += w_specs
    args = list(a_list) + [w_stack] * len(a_list)
    tile = pl.BlockSpec((tm, tn), lambda i, j: (i, j))
    rows = pl.BlockSpec((tm, LANES), lambda i, j: (i, 0))
    if residual is not None:
        in_specs.append(tile)
        args.append(residual)
    if row_scale is not None:
        in_specs.append(rows)
        args.append(row_scale)
    out_specs, out_shape, scratch = tile, jax.ShapeDtypeStruct((M, N), out_dtype), []
    if next_gain is not None:
        in_specs.append(pl.BlockSpec((1, tn), lambda i, j: (0, j)))
        args.append(next_gain.reshape(1, N).astype(F32))
        out_specs = [tile, tile, rows]
        out_shape = [out_shape, jax.ShapeDtypeStruct((M, N), BF16), jax.ShapeDtypeStruct((M, LANES), F32)]
        scratch = [pltpu.VMEM((tm, LANES), F32)]
    return pl.pallas_call(
        functools.partial(_mm_body, n_a=len(a_list), has_res=residual is not None,
                          has_scale=row_scale is not None, has_next=next_gain is not None, n_cols=N),
        grid=(M // tm, N // tn),
        in_specs=in_specs,
        out_specs=out_specs,
        out_shape=out_shape,
        scratch_shapes=scratch,
        compiler_params=_params("arbitrary", "arbitrary"),
        name="matmul",
    )(*args)


def _norm_factors_body(x_ref, g_ref, xg_ref, r_ref):
    x = x_ref[...]
    xg_ref[...] = (x * g_ref[...]).astype(BF16)
    ms = jnp.mean(x * x, axis=-1, keepdims=True)
    r_ref[...] = jnp.broadcast_to(lax.rsqrt(ms + NORM_EPS), r_ref.shape)


def _norm_factors(x, gain, tm=256):
    M, D = x.shape
    tm = min(tm, M)
    return pl.pallas_call(
        _norm_factors_body,
        grid=(M // tm,),
        in_specs=[pl.BlockSpec((tm, D), lambda i: (i, 0)),
                  pl.BlockSpec((1, D), lambda i: (0, 0))],
        out_specs=[pl.BlockSpec((tm, D), lambda i: (i, 0)),
                   pl.BlockSpec((tm, LANES), lambda i: (i, 0))],
        out_shape=[jax.ShapeDtypeStruct((M, D), BF16), jax.ShapeDtypeStruct((M, LANES), F32)],
        compiler_params=_params("parallel"),
        name="norm_factors",
    )(x, gain.reshape(1, D).astype(F32))


def _t5_bucket(dist):
    max_exact = REL_BUCKETS // 2
    safe = np.maximum(dist, 1).astype(np.float32)
    large = max_exact + (np.log(safe / max_exact) / np.log(REL_MAX_DIST / max_exact)
                         * (REL_BUCKETS - max_exact)).astype(np.int32)
    large = np.minimum(large, REL_BUCKETS - 1)
    return np.where(dist < max_exact, dist, large).astype(np.int32)


def _attn_bucket_table():
    B = ATTN_BLOCK
    i = np.arange(B)[:, None]
    j = np.arange(2 * B)[None, :]
    rel = i + B - j
    band = (rel >= 0) & (rel <= B)
    tabs = [np.where(band, _t5_bucket(np.clip(rel, 0, None) * d), -1) for d in DILATIONS]
    return np.stack(tabs).astype(np.int32)


def _attn_body(relb_ref, bkt_ref, q_ref, k_ref, v_ref, z_ref, qg_ref, kg_ref,
               o_ref, bias_ref, qn_ref, q4_ref, kn_ref, k4_ref, vn_ref, v4_ref,
               o1_ref, m1_ref, l1_ref, o2_ref, m2_ref, l2_ref, o3_ref, m3_ref, l3_ref, out_ref):
    B = ATTN_BLOCK
    C = ATTN_CHUNK
    R4 = DILATIONS[1]
    Q = C // R4
    h = pl.program_id(0)
    c = pl.program_id(1)
    n_dil = len(DILATIONS)
    cur = c % 2
    prv = 1 - cur

    @pl.when(c == 0)
    def _():
        col = lax.broadcasted_iota(jnp.int32, (B, 2 * B), 1)
        for t in range(n_dil):
            bkt = bkt_ref[t]
            bias = jnp.full(bkt.shape, -jnp.inf, F32)
            for b in range(REL_BUCKETS):
                bias = jnp.where(bkt == b, relb_ref[b, h] * LOG2E, bias)
            bias_ref[t] = bias
            bias_ref[t + n_dil] = jnp.where(col >= B, bias, -jnp.inf)
        kn_ref[1] = jnp.zeros(kn_ref.shape[1:], F32)
        vn_ref[1] = jnp.zeros(vn_ref.shape[1:], F32)
        k4_ref[1] = jnp.zeros(k4_ref.shape[1:], F32)
        v4_ref[1] = jnp.zeros(v4_ref.shape[1:], F32)

    def _norm(x, g):
        ms = jnp.mean(x * x, axis=-1, keepdims=True)
        return x * lax.rsqrt(ms + NORM_EPS) * g

    qn_ref[...] = _norm(q_ref[...], qg_ref[...]) * (ATTN_HEAD_DIM ** -0.5 * LOG2E)
    kn_ref[cur] = _norm(k_ref[...], kg_ref[...])
    vn_ref[cur] = v_ref[...]
    for r4 in range(R4):
        rows = pl.ds(r4, Q, stride=R4)
        q4_ref[r4] = qn_ref[rows, :]
        k4_ref[cur, r4] = kn_ref[cur, rows, :]
        v4_ref[cur, r4] = vn_ref[cur, rows, :]

    first_chunk = jnp.where(c == 0, 1, 0)

    def block(q, k, v, bias):
        s = lax.dot_general(q.astype(BF16), k.astype(BF16), (((1,), (1,)), ((), ())),
                            preferred_element_type=F32) + bias
        m = jnp.max(s, axis=-1, keepdims=True)
        p = jnp.exp2(s - m).astype(BF16)
        v_aug = jnp.concatenate([v.astype(BF16), jnp.ones((2 * B, LANES), BF16)], axis=1)
        o_aug = jnp.dot(p, v_aug, preferred_element_type=F32)
        return o_aug[:, :ATTN_HEAD_DIM], m, o_aug[:, ATTN_HEAD_DIM:]

    def bcast(x):
        return jnp.broadcast_to(x, (B, LANES))

    for b in range(C // B):
        rb = slice(b * B, (b + 1) * B)
        if b == 0:
            ka, va = kn_ref[prv, C - B:C, :], vn_ref[prv, C - B:C, :]
            bias = bias_ref[n_dil * first_chunk]
        else:
            ka, va = kn_ref[cur, (b - 1) * B:b * B, :], vn_ref[cur, (b - 1) * B:b * B, :]
            bias = bias_ref[0]
        k = jnp.concatenate([ka, kn_ref[cur, rb, :]], axis=0)
        v = jnp.concatenate([va, vn_ref[cur, rb, :]], axis=0)
        o, m, l = block(qn_ref[rb, :], k, v, bias)
        o1_ref[rb, :] = o
        m1_ref[rb, :] = bcast(m)
        l1_ref[rb, :] = l

    for sub in range(Q // B):
        rb = slice(sub * B, (sub + 1) * B)
        bias = bias_ref[1 + n_dil * first_chunk] if sub == 0 else bias_ref[1]
        for r4 in range(R4):
            if sub == 0:
                ka, va = k4_ref[prv, r4, Q - B:Q, :], v4_ref[prv, r4, Q - B:Q, :]
            else:
                ra = slice((sub - 1) * B, sub * B)
                ka, va = k4_ref[cur, r4, ra, :], v4_ref[cur, r4, ra, :]
            k = jnp.concatenate([ka, k4_ref[cur, r4, rb, :]], axis=0)
            v = jnp.concatenate([va, v4_ref[cur, r4, rb, :]], axis=0)
            o, m, l = block(q4_ref[r4, rb, :], k, v, bias)
            o2_ref[r4, rb, :] = o
            m2_ref[r4, rb, :] = bcast(m)
            l2_ref[r4, rb, :] = l

    bias = bias_ref[2 + n_dil * first_chunk]
    for o4 in range(Q // B):
        rows = pl.ds(o4, B, stride=R4)
        for r4 in range(R4):
            k = jnp.concatenate([k4_ref[prv, r4, rows, :], k4_ref[cur, r4, rows, :]], axis=0)
            v = jnp.concatenate([v4_ref[prv, r4, rows, :], v4_ref[cur, r4, rows, :]], axis=0)
            o, m, l = block(q4_ref[r4, rows, :], k, v, bias)
            o3_ref[r4, rows, :] = o
            m3_ref[r4, rows, :] = bcast(m)
            l3_ref[r4, rows, :] = l

    def body_merge(sub, carry):
        r0 = pl.multiple_of(sub * B, B)
        for r4 in range(R4):
            nat = pl.ds(sub * (B * R4) + r4, B, stride=R4)
            m1, m2, m3 = m1_ref[nat, :], m2_ref[r4, pl.ds(r0, B), :], m3_ref[r4, pl.ds(r0, B), :]
            mx = jnp.maximum(jnp.maximum(m1, m2), m3)
            w1, w2, w3 = jnp.exp2(m1 - mx), jnp.exp2(m2 - mx), jnp.exp2(m3 - mx)
            num = (o1_ref[nat, :] * w1 + o2_ref[r4, pl.ds(r0, B), :] * w2
                   + o3_ref[r4, pl.ds(r0, B), :] * w3)
            den = (l1_ref[nat, :] * w1 + l2_ref[r4, pl.ds(r0, B), :] * w2
                   + l3_ref[r4, pl.ds(r0, B), :] * w3)
            out_ref[nat, :] = num / den
        return carry

    lax.fori_loop(0, Q // B, body_merge, 0)

    o_ref[...] = (out_ref[...] * _silu(z_ref[...])).astype(o_ref.dtype)


def _attention(proj, rel_bias, q_gain, k_gain, n_heads, z_col):
    S = proj.shape[0]
    C, B, Dh = ATTN_CHUNK, ATTN_BLOCK, ATTN_HEAD_DIM
    assert S % C == 0
    H = n_heads
    bkt = jnp.asarray(_attn_bucket_table())
    R4 = DILATIONS[1]
    blk = lambda f: pl.BlockSpec((C, Dh), f)
    nat = pltpu.VMEM((C, Dh), F32)
    mod4 = pltpu.VMEM((R4, C // R4, Dh), F32)
    return pl.pallas_call(
        _attn_body,
        grid=(H, S // C),
        in_specs=[
            pl.BlockSpec(memory_space=pltpu.SMEM),
            pl.BlockSpec(bkt.shape, lambda h, c: (0, 0, 0)),
            blk(lambda h, c: (c, h)),
            blk(lambda h, c: (c, H + h)),
            blk(lambda h, c: (c, 2 * H + h)),
            blk(lambda h, c: (c, z_col + h)),
            pl.BlockSpec((1, Dh), lambda h, c: (0, 0)),
            pl.BlockSpec((1, Dh), lambda h, c: (0, 0)),
        ],
        out_specs=blk(lambda h, c: (c, h)),
        out_shape=jax.ShapeDtypeStruct((S, H * Dh), BF16),
        scratch_shapes=[
            pltpu.VMEM((2 * len(DILATIONS), B, 2 * B), F32),
            nat, mod4,
            pltpu.VMEM((2, C, Dh), F32), pltpu.VMEM((2, R4, C // R4, Dh), F32),
            pltpu.VMEM((2, C, Dh), F32), pltpu.VMEM((2, R4, C // R4, Dh), F32),
            nat, nat, nat,
            mod4, mod4, mod4,
            mod4, mod4, mod4,
            nat,
        ],
        compiler_params=_params("arbitrary", "arbitrary"),
        name="dilated_attention",
    )(rel_bias.astype(F32), bkt, proj, proj, proj, proj,
      q_gain.reshape(1, Dh).astype(F32), k_gain.reshape(1, Dh).astype(F32))


def _pool_body(u_ref, halo_ref, z_ref, w_ref, sc_ref, o_ref, *, group_dim):
    T = u_ref.shape[0]
    i = pl.program_id(0)
    G = len(POOL_WINDOWS)
    halo_on = jnp.where(i > 0, 1.0, 0.0)
    pos = (i * T + lax.broadcasted_iota(jnp.int32, (T, 1), 0) + 1).astype(F32)
    for g, w in enumerate(POOL_WINDOWS):
        cols = slice(g * group_dim, (g + 1) * group_dim)
        x = u_ref[:, cols]
        e = jnp.concatenate([halo_ref[:, cols] * halo_on, x], axis=0)
        width = 1
        while width < w:
            e = e[width:, :] + e[:-width, :]
            width *= 2
        off = POOL_HALO - (w - 1)
        win = e[off:off + T, :]
        y = win / jnp.minimum(pos, float(w)) - x
        yp = jnp.dot(y.astype(BF16), w_ref[g].astype(BF16), preferred_element_type=F32)
        o_ref[:, cols] = (yp * sc_ref[:, cols] * _silu(z_ref[:, cols])).astype(o_ref.dtype)


def _pool(proj, pool_w, pool_scale, u_col, z_col, tile=256):
    S = proj.shape[0]
    G, Cg, _ = pool_w.shape
    P = G * Cg
    T = min(tile, S)
    assert u_col % P == 0 and z_col % P == 0 and T % POOL_HALO == 0
    return pl.pallas_call(
        functools.partial(_pool_body, group_dim=Cg),
        grid=(S // T,),
        in_specs=[
            pl.BlockSpec((T, P), lambda i: (i, u_col // P)),
            pl.BlockSpec((POOL_HALO, P), lambda i: (jnp.maximum(i * (T // POOL_HALO) - 1, 0), u_col // P)),
            pl.BlockSpec((T, P), lambda i: (i, z_col // P)),
            pl.BlockSpec((G, Cg, Cg), lambda i: (0, 0, 0)),
            pl.BlockSpec((1, P), lambda i: (0, 0)),
        ],
        out_specs=pl.BlockSpec((T, P), lambda i: (i, 0)),
        out_shape=jax.ShapeDtypeStruct((S, P), BF16),
        compiler_params=_params("parallel"),
        name="multiscale_pool",
    )(proj, proj, proj, pool_w.astype(F32), pool_scale.reshape(1, P).astype(F32))


def _front_body(xm_ref, halo_ref, cw_ref, cb_ref, wq_ref, wk_ref, wv_ref, wif_ref, bif_ref,
                xc_ref, q_ref, k_ref, v_ref, g_ref):
    T, TC = xm_ref.shape
    i = pl.program_id(0)
    j = pl.program_id(1)
    xm = xm_ref[...]
    halo = halo_ref[...] * jnp.where(i > 0, 1.0, 0.0)
    e = jnp.concatenate([halo, xm], axis=0)
    conv = cb_ref[...]
    for t in range(CONV_WIDTH):
        off = CONV_HALO - (CONV_WIDTH - 1) + t
        conv = conv + e[off:off + T, :] * cw_ref[t:t + 1, :]
    xc = _silu(conv)
    xc_ref[...] = xc
    xcb = xc.astype(BF16)
    xmb = xm.astype(BF16)
    gates = jnp.zeros(g_ref.shape, F32)
    W = wq_ref.shape[-1]
    for (src, w_ref, o_ref, p) in ((xcb, wq_ref, q_ref, 0), (xcb, wk_ref, k_ref, 1), (xmb, wv_ref, v_ref, 2)):
        for n in range(TC // W):
            cols = slice(n * W, (n + 1) * W)
            y = jnp.dot(src[:, cols], w_ref[n], preferred_element_type=F32)
            yb = y.astype(BF16)
            o_ref[:, cols] = yb
            gates = gates + jnp.dot(yb, wif_ref[p, cols, :], preferred_element_type=F32)

    @pl.when(j == 0)
    def _():
        g_ref[...] = bif_ref[...] + gates

    @pl.when(j > 0)
    def _():
        g_ref[...] += gates


def _block_diag_dense(w, width):
    nb, bs, _ = w.shape
    per = width // bs
    wg = w.reshape(nb // per, per, bs, bs)
    eye = jnp.eye(per, dtype=w.dtype)
    dense = jnp.einsum('gpcd,pq->gpcqd', wg, eye)
    return dense.reshape(nb // per, width, width).astype(BF16)


def _mlstm_front(up, conv_w, conv_b, wq, wk, wv, w_if, b_if, tile=512, tcol=1024):
    S = up.shape[0]
    E = conv_w.shape[1]
    T, TC = min(tile, S), min(tcol, E)
    NG = w_if.shape[1]
    W = min(QKV_TILE, TC)
    nt = TC // W
    wdense = [_block_diag_dense(w, W) for w in (wq, wk, wv)]
    wif = w_if.reshape(3, E, NG).astype(BF16)
    row = lambda: pl.BlockSpec((T, TC), lambda i, j: (i, j))
    return pl.pallas_call(
        _front_body,
        grid=(S // T, E // TC),
        in_specs=[
            row(),
            pl.BlockSpec((CONV_HALO, TC), lambda i, j: (jnp.maximum(i * (T // CONV_HALO) - 1, 0), j)),
            pl.BlockSpec((CONV_WIDTH, TC), lambda i, j: (0, j)),
            pl.BlockSpec((1, TC), lambda i, j: (0, j)),
            pl.BlockSpec((nt, W, W), lambda i, j: (j, 0, 0)),
            pl.BlockSpec((nt, W, W), lambda i, j: (j, 0, 0)),
            pl.BlockSpec((nt, W, W), lambda i, j: (j, 0, 0)),
            pl.BlockSpec((3, TC, NG), lambda i, j: (0, j, 0)),
            pl.BlockSpec((1, NG), lambda i, j: (0, 0)),
        ],
        out_specs=[row(), row(), row(), row(), pl.BlockSpec((T, NG), lambda i, j: (i, 0))],
        out_shape=[jax.ShapeDtypeStruct((S, E), F32)] + [jax.ShapeDtypeStruct((S, E), BF16)] * 3
        + [jax.ShapeDtypeStruct((S, NG), F32)],
        compiler_params=_params("parallel", "arbitrary"),
        name="mlstm_front",
    )(up, up, conv_w.astype(F32), conv_b.reshape(1, E).astype(F32), *wdense, wif,
      b_if.reshape(1, NG).astype(F32))


def _mlstm_body(q_ref, k_ref, v_ref, ig_ref, fg_ref, op_ref, xc_ref, z_ref, gn_ref, sk_ref,
                o_ref, c_ref, cb_ref, n_ref, m_ref):
    L = q_ref.shape[0]
    G, DK, _ = c_ref.shape
    c = pl.program_id(1)

    @pl.when(c == 0)
    def _():
        c_ref[...] = jnp.zeros_like(c_ref)
        cb_ref[...] = jnp.zeros_like(cb_ref)
        n_ref[...] = jnp.zeros_like(n_ref)
        m_ref[...] = jnp.full(m_ref.shape, -1e30, F32)

    ri = lax.broadcasted_iota(jnp.int32, (L, L), 0)
    cj = lax.broadcasted_iota(jnp.int32, (L, L), 1)
    lane8 = lax.broadcasted_iota(jnp.int32, (8, L), 1)
    scale = DK ** -0.5

    for hh in range(G):
        cols = slice(hh * DK, (hh + 1) * DK)
        i_row = ig_ref[hh, 0]
        f_row = fg_ref[hh, 0]
        lf_row = jnp.minimum(f_row, 0.0) - jnp.log1p(jnp.exp(-jnp.abs(f_row)))
        b8 = jnp.broadcast_to(lf_row, (8, L))
        sh = 1
        while sh < L:
            b8 = b8 + jnp.where(lane8 >= sh, pltpu.roll(b8, sh, 1), 0.0)
            sh *= 2
        b_row = b8[0:1, :]
        stacked = jnp.where(ri == 0, jnp.broadcast_to(b_row, (L, L)),
                            jnp.where(ri == 1, jnp.broadcast_to(i_row, (L, L)), 0.0))
        stacked_t = stacked.T
        b_col = stacked_t[:, 0:1]
        i_col = stacked_t[:, 1:2]

        m_prev = m_ref[hh, 0:1, 0:1]
        log_d = jnp.where(cj <= ri, b_col - b_row + i_row, -jnp.inf)
        log_inter = b_col + m_prev
        m_t = jnp.maximum(jnp.max(log_d, axis=-1, keepdims=True), log_inter)
        dmat = jnp.exp(log_d - m_t) * scale
        g = jnp.exp(log_inter - m_t)

        q = q_ref[:, cols]
        k = k_ref[:, cols]
        v = v_ref[:, cols]
        s = lax.dot_general(q, k, (((1,), (1,)), ((), ())), preferred_element_type=F32) * dmat
        inter = jnp.dot(q, cb_ref[hh], preferred_element_type=F32)
        num = jnp.dot(s.astype(BF16), v, preferred_element_type=F32) + g * inter
        qn = jnp.sum(q.astype(F32) * n_ref[hh], axis=-1, keepdims=True)
        den = jnp.sum(s, axis=-1, keepdims=True) + g * qn
        hc = num / jnp.maximum(jnp.abs(den), jnp.exp(-m_t))

        m_new = m_t[L - 1:L, :]
        b_last = b_col[L - 1:L, :]
        decay = jnp.exp(b_last + m_prev - m_new)
        w_col = jnp.exp(b_last - b_col + i_col - m_new) * scale
        vw = (v.astype(F32) * w_col).astype(BF16)
        upd = lax.dot_general(k, vw, (((0,), (0,)), ((), ())), preferred_element_type=F32)
        c_new = upd + c_ref[hh] * decay
        c_ref[hh] = c_new
        cb_ref[hh] = c_new.astype(BF16)
        n_ref[hh] = n_ref[hh] * decay + jnp.sum(k.astype(F32) * w_col, axis=0, keepdims=True)
        m_ref[hh] = jnp.broadcast_to(m_new, m_ref.shape[1:])

        mu = jnp.mean(hc, axis=-1, keepdims=True)
        ctr = hc - mu
        var = jnp.mean(ctr * ctr, axis=-1, keepdims=True)
        hn = ctr * lax.rsqrt(var + NORM_EPS)
        cell = (1.0 / (1.0 + jnp.exp(-op_ref[:, cols]))) * (hn * gn_ref[:, cols])
        o_ref[:, cols] = ((cell + sk_ref[:, cols] * xc_ref[:, cols])
                          * _silu(z_ref[:, cols])).astype(o_ref.dtype)


def _mlstm(q, k, v, gates, up, xc, gn, skip):
    S, E = q.shape
    H, L, G = MLSTM_HEADS, MLSTM_CHUNK, MLSTM_GROUP
    DH = E // H
    NCH = S // L
    HG = H // G
    gt = gates.T.reshape(2 * H, NCH, 1, L)
    blk = lambda col0: pl.BlockSpec((L, G * DH), lambda h, c: (c, col0 + h))
    vec = pl.BlockSpec((1, G * DH), lambda h, c: (0, h))
    return pl.pallas_call(
        _mlstm_body,
        grid=(HG, NCH),
        in_specs=[
            blk(0), blk(0), blk(0),
            pl.BlockSpec((G, 1, 1, L), lambda h, c: (h, c, 0, 0)),
            pl.BlockSpec((G, 1, 1, L), lambda h, c: (HG + h, c, 0, 0)),
            blk(2 * HG), blk(0), blk(HG),
            vec, vec,
        ],
        out_specs=blk(0),
        out_shape=jax.ShapeDtypeStruct((S, E), BF16),
        scratch_shapes=[
            pltpu.VMEM((G, DH, DH), F32),
            pltpu.VMEM((G, DH, DH), BF16),
            pltpu.VMEM((G, 1, DH), F32),
            pltpu.VMEM((G, 8, LANES), F32),
        ],
        compiler_params=_params("arbitrary", "arbitrary"),
        name="mlstm_chunkwise",
    )(q, k, v, gt, gt, up, xc, up, gn.reshape(1, E).astype(F32), skip.reshape(1, E).astype(F32))


def _even_layer(h, xg, r, j, rel_bias, w_in, q_gain, k_gain, pool_w, pool_scale, w_out, next_gain):
    mix = w_out.shape[1]
    pool_width = pool_w.shape[0] * pool_w.shape[1]
    attn_width = mix - pool_width
    n_heads = attn_width // ATTN_HEAD_DIM
    u_col = 3 * attn_width
    z_col = u_col + pool_width
    proj = _matmul([xg], w_in, j, row_scale=r, tm=IN_PROJ_ROWS)
    attn = _attention(proj, rel_bias, q_gain, k_gain, n_heads, z_col // ATTN_HEAD_DIM)
    pool = _pool(proj, pool_w, pool_scale, u_col, z_col + attn_width)
    return _matmul([attn, pool], w_out, j, residual=h, next_gain=next_gain, tn=OUT_PROJ_COLS)


def _odd_layer(h, xg, r, j, w_up, conv_w, conv_b, wq, wk, wv, w_if, b_if, gn, skip, w_down,
               next_gain):
    up = _matmul([xg], w_up, j, row_scale=r, tm=IN_PROJ_ROWS)
    xc, q, k, v, gates = _mlstm_front(up, conv_w, conv_b, wq, wk, wv, w_if, b_if)
    out = _mlstm(q, k, v, gates, up, xc, gn, skip)
    return _matmul([out], w_down, j, residual=h, next_gain=next_gain, tn=OUT_PROJ_COLS)


def kernel(x, rel_bias, e_norm, e_w_in, e_q_gain, e_k_gain, e_pool_w, e_pool_scale, e_w_out,
           o_norm, o_w_up, o_conv_w, o_conv_b, o_wq, o_wk, o_wv, o_w_if, o_b_if, o_gn, o_skip,
           o_w_down):
    B, S, D = x.shape
    depth = e_norm.shape[0] + o_norm.shape[0]
    outs = []
    for b in range(B):
        h = x[b]
        gains = [(e_norm if layer % 2 == 0 else o_norm)[layer // 2] for layer in range(depth)]
        xg, r = _norm_factors(h, gains[0])
        for layer in range(depth):
            j = layer // 2
            next_gain = gains[layer + 1] if layer + 1 < depth else None
            if layer % 2 == 0:
                res = _even_layer(h, xg, r, j, rel_bias, e_w_in, e_q_gain[j], e_k_gain[j],
                                  e_pool_w[j], e_pool_scale[j], e_w_out, next_gain)
            else:
                res = _odd_layer(h, xg, r, j, o_w_up, o_conv_w[j], o_conv_b[j], o_wq[j], o_wk[j],
                                 o_wv[j], o_w_if[j], o_b_if[j], o_gn[j], o_skip[j], o_w_down,
                                 next_gain)
            h, xg, r = res if next_gain is not None else (res, None, None)
        outs.append(h)
    return jnp.stack(outs)
```

```python
import functools

import numpy as np
import jax
import jax.numpy as jnp
from jax import lax
from jax.experimental import pallas as pl
from jax.experimental.pallas import tpu as pltpu

F32 = jnp.float32
BF16 = jnp.bfloat16

NORM_EPS = 1e-6
IN_PROJ_ROWS = 2048
OUT_PROJ_COLS = 512
LOG2E = 1.4426950408889634
LANES = 128
ATTN_HEAD_DIM = 128
ATTN_BLOCK = 128
DILATIONS = (1, 4, 16)
ATTN_CHUNK = ATTN_BLOCK * DILATIONS[-1]
POOL_WINDOWS = (2, 4, 8, 16)
POOL_HALO = 16
REL_BUCKETS = 32
REL_MAX_DIST = 2048
MLSTM_HEADS = 8
MLSTM_CHUNK = 256
MLSTM_GROUP = 2
CONV_WIDTH = 4
CONV_HALO = 8
QKV_TILE = 128
VMEM_LIMIT = 56 * 1024 * 1024


def _params(*sem):
    return pltpu.CompilerParams(dimension_semantics=sem, vmem_limit_bytes=VMEM_LIMIT)


def _silu(x):
    return x * (1.0 / (1.0 + jnp.exp(-x)))


def _mm_body(*refs, n_a, has_res, has_scale, has_next, n_cols):
    a_refs = refs[:n_a]
    w_refs = refs[n_a:2 * n_a]
    p = 2 * n_a
    res_ref = refs[p] if has_res else None
    p += int(has_res)
    scale_ref = refs[p] if has_scale else None
    p += int(has_scale)
    gain_ref = refs[p] if has_next else None
    p += int(has_next)
    o_ref = refs[p]
    acc = None
    for a_ref, w_ref in zip(a_refs, w_refs):
        w = w_ref[...]
        d = jnp.dot(a_ref[...], w if w.dtype == BF16 else w.astype(BF16),
                    preferred_element_type=F32)
        acc = d if acc is None else acc + d
    if has_scale:
        acc = acc * scale_ref[:, 0:1]
    if has_res:
        acc = res_ref[...] + acc
    o_ref[...] = acc.astype(o_ref.dtype)
    if has_next:
        xg_ref, r_ref, ssq_ref = refs[p + 1], refs[p + 2], refs[p + 3]
        j = pl.program_id(1)
        xg_ref[...] = (acc * gain_ref[...]).astype(BF16)
        part = jnp.broadcast_to(jnp.sum(acc * acc, axis=-1, keepdims=True), ssq_ref.shape)

        @pl.when(j == 0)
        def _():
            ssq_ref[...] = part

        @pl.when(j > 0)
        def _():
            ssq_ref[...] += part

        @pl.when(j == pl.num_programs(1) - 1)
        def _():
            r_ref[...] = lax.rsqrt(ssq_ref[...] * (1.0 / n_cols) + NORM_EPS)


def _matmul(a_list, w_stack, layer, residual=None, row_scale=None, next_gain=None,
            out_dtype=F32, tm=1024, tn=512):
    M = a_list[0].shape[0]
    K, N = w_stack.shape[-2:]
    tm, tn = min(tm, M), min(tn, N)
    assert M % tm == 0 and N % tn == 0 and sum(a.shape[1] for a in a_list) == K
    in_specs, w_specs, row0 = [], [], 0
    for a in a_list:
        kp = a.shape[1]
        assert row0 % kp == 0
        in_specs.append(pl.BlockSpec((tm, kp), lambda i, j: (i, 0), pipeline_mode=pl.Buffered(1)))
        if w_stack.ndim == 3:
            w_specs.append(pl.BlockSpec((None, kp, tn), lambda i, j, rb=row0 // kp: (layer, rb, j)))
        else:
            w_specs.append(pl.BlockSpec((kp, tn), lambda i, j, rb=row0 // kp: (rb, j)))
        row0 += kp
    in_specs += w_specs
    args = list(a_list) + [w_stack] * len(a_list)
    tile = pl.BlockSpec((tm, tn), lambda i, j: (i, j))
    rows = pl.BlockSpec((tm, LANES), lambda i, j: (i, 0))
    if residual is not None:
        in_specs.append(tile)
        args.append(residual)
    if row_scale is not None:
        in_specs.append(rows)
        args.append(row_scale)
    out_specs, out_shape, scratch = tile, jax.ShapeDtypeStruct((M, N), out_dtype), []
    if next_gain is not None:
        in_specs.append(pl.BlockSpec((1, tn), lambda i, j: (0, j)))
        args.append(next_gain.reshape(1, N).astype(F32))
        out_specs = [tile, tile, rows]
        out_shape = [out_shape, jax.ShapeDtypeStruct((M, N), BF16), jax.ShapeDtypeStruct((M, LANES), F32)]
        scratch = [pltpu.VMEM((tm, LANES), F32)]
    return pl.pallas_call(
        functools.partial(_mm_body, n_a=len(a_list), has_res=residual is not None,
                          has_scale=row_scale is not None, has_next=next_gain is not None, n_cols=N),
        grid=(M // tm, N // tn),
        in_specs=in_specs,
        out_specs=out_specs,
        out_shape=out_shape,
        scratch_shapes=scratch,
        compiler_params=_params("arbitrary", "arbitrary"),
        name="matmul",
    )(*args)


def _side_cast_specs(w_stack, layer, n_steps, step_of):
    K, N = w_stack.shape[1:]
    n_blocks = 1 << (n_steps.bit_length() - 1)
    while K % n_blocks or (K // n_blocks) % 16:
        n_blocks //= 2
    rb = K // n_blocks
    idx = lambda *g: jnp.minimum(step_of(*g), n_blocks - 1)
    return (pl.BlockSpec((None, rb, N), lambda *g: (layer, idx(*g), 0)),
            pl.BlockSpec((rb, N), lambda *g: (idx(*g), 0)), n_blocks)


def _side_cast(wf_ref, wb_ref, step, n_blocks):
    @pl.when(step < n_blocks)
    def _():
        wb_ref[...] = wf_ref[...].astype(BF16)


def _norm_factors_body(x_ref, g_ref, xg_ref, r_ref):
    x = x_ref[...]
    xg_ref[...] = (x * g_ref[...]).astype(BF16)
    ms = jnp.mean(x * x, axis=-1, keepdims=True)
    r_ref[...] = jnp.broadcast_to(lax.rsqrt(ms + NORM_EPS), r_ref.shape)


def _norm_factors(x, gain, tm=256):
    M, D = x.shape
    tm = min(tm, M)
    return pl.pallas_call(
        _norm_factors_body,
        grid=(M // tm,),
        in_specs=[pl.BlockSpec((tm, D), lambda i: (i, 0)),
                  pl.BlockSpec((1, D), lambda i: (0, 0))],
        out_specs=[pl.BlockSpec((tm, D), lambda i: (i, 0)),
                   pl.BlockSpec((tm, LANES), lambda i: (i, 0))],
        out_shape=[jax.ShapeDtypeStruct((M, D), BF16), jax.ShapeDtypeStruct((M, LANES), F32)],
        compiler_params=_params("parallel"),
        name="norm_factors",
    )(x, gain.reshape(1, D).astype(F32))


def _t5_bucket(dist):
    max_exact = REL_BUCKETS // 2
    safe = np.maximum(dist, 1).astype(np.float32)
    large = max_exact + (np.log(safe / max_exact) / np.log(REL_MAX_DIST / max_exact)
                         * (REL_BUCKETS - max_exact)).astype(np.int32)
    large = np.minimum(large, REL_BUCKETS - 1)
    return np.where(dist < max_exact, dist, large).astype(np.int32)


def _attn_bucket_table():
    B = ATTN_BLOCK
    i = np.arange(B)[:, None]
    j = np.arange(2 * B)[None, :]
    rel = i + B - j
    band = (rel >= 0) & (rel <= B)
    tabs = [np.where(band, _t5_bucket(np.clip(rel, 0, None) * d), -1) for d in DILATIONS]
    return np.stack(tabs).astype(np.int32)


def _attn_body(relb_ref, bkt_ref, q_ref, k_ref, v_ref, z_ref, qg_ref, kg_ref, wf_ref,
               o_ref, wb_ref, bias_ref, qn_ref, q4_ref, kn_ref, k4_ref, vn_ref, v4_ref,
               o1_ref, m1_ref, l1_ref, o2_ref, m2_ref, l2_ref, o3_ref, m3_ref, l3_ref, out_ref,
               *, cast_blocks):
    B = ATTN_BLOCK
    C = ATTN_CHUNK
    R4 = DILATIONS[1]
    Q = C // R4
    h = pl.program_id(0)
    c = pl.program_id(1)
    n_dil = len(DILATIONS)
    cur = c % 2
    prv = 1 - cur

    @pl.when(c == 0)
    def _():
        col = lax.broadcasted_iota(jnp.int32, (B, 2 * B), 1)
        for t in range(n_dil):
            bkt = bkt_ref[t]
            bias = jnp.full(bkt.shape, -jnp.inf, F32)
            for b in range(REL_BUCKETS):
                bias = jnp.where(bkt == b, relb_ref[b, h] * LOG2E, bias)
            bias_ref[t] = bias
            bias_ref[t + n_dil] = jnp.where(col >= B, bias, -jnp.inf)
        kn_ref[1] = jnp.zeros(kn_ref.shape[1:], F32)
        vn_ref[1] = jnp.zeros(vn_ref.shape[1:], F32)
        k4_ref[1] = jnp.zeros(k4_ref.shape[1:], F32)
        v4_ref[1] = jnp.zeros(v4_ref.shape[1:], F32)

    def _norm(x, g):
        ms = jnp.mean(x * x, axis=-1, keepdims=True)
        return x * lax.rsqrt(ms + NORM_EPS) * g

    qn_ref[...] = _norm(q_ref[...], qg_ref[...]) * (ATTN_HEAD_DIM ** -0.5 * LOG2E)
    kn_ref[cur] = _norm(k_ref[...], kg_ref[...])
    vn_ref[cur] = v_ref[...]
    for r4 in range(R4):
        rows = pl.ds(r4, Q, stride=R4)
        q4_ref[r4] = qn_ref[rows, :]
        k4_ref[cur, r4] = kn_ref[cur, rows, :]
        v4_ref[cur, r4] = vn_ref[cur, rows, :]

    first_chunk = jnp.where(c == 0, 1, 0)

    def block(q, k, v, bias):
        s = lax.dot_general(q.astype(BF16), k.astype(BF16), (((1,), (1,)), ((), ())),
                            preferred_element_type=F32) + bias
        m = jnp.max(s, axis=-1, keepdims=True)
        p = jnp.exp2(s - m).astype(BF16)
        v_aug = jnp.concatenate([v.astype(BF16), jnp.ones((2 * B, LANES), BF16)], axis=1)
        o_aug = jnp.dot(p, v_aug, preferred_element_type=F32)
        return o_aug[:, :ATTN_HEAD_DIM], m, o_aug[:, ATTN_HEAD_DIM:]

    def bcast(x):
        return jnp.broadcast_to(x, (B, LANES))

    for b in range(C // B):
        rb = slice(b * B, (b + 1) * B)
        if b == 0:
            ka, va = kn_ref[prv, C - B:C, :], vn_ref[prv, C - B:C, :]
            bias = bias_ref[n_dil * first_chunk]
        else:
            ka, va = kn_ref[cur, (b - 1) * B:b * B, :], vn_ref[cur, (b - 1) * B:b * B, :]
            bias = bias_ref[0]
        k = jnp.concatenate([ka, kn_ref[cur, rb, :]], axis=0)
        v = jnp.concatenate([va, vn_ref[cur, rb, :]], axis=0)
        o, m, l = block(qn_ref[rb, :], k, v, bias)
        o1_ref[rb, :] = o
        m1_ref[rb, :] = bcast(m)
        l1_ref[rb, :] = l

    for sub in range(Q // B):
        rb = slice(sub * B, (sub + 1) * B)
        bias = bias_ref[1 + n_dil * first_chunk] if sub == 0 else bias_ref[1]
        for r4 in range(R4):
            if sub == 0:
                ka, va = k4_ref[prv, r4, Q - B:Q, :], v4_ref[prv, r4, Q - B:Q, :]
            else:
                ra = slice((sub - 1) * B, sub * B)
                ka, va = k4_ref[cur, r4, ra, :], v4_ref[cur, r4, ra, :]
            k = jnp.concatenate([ka, k4_ref[cur, r4, rb, :]], axis=0)
            v = jnp.concatenate([va, v4_ref[cur, r4, rb, :]], axis=0)
            o, m, l = block(q4_ref[r4, rb, :], k, v, bias)
            o2_ref[r4, rb, :] = o
            m2_ref[r4, rb, :] = bcast(m)
            l2_ref[r4, rb, :] = l

    bias = bias_ref[2 + n_dil * first_chunk]
    for o4 in range(Q // B):
        rows = pl.ds(o4, B, stride=R4)
        for r4 in range(R4):
            k = jnp.concatenate([k4_ref[prv, r4, rows, :], k4_ref[cur, r4, rows, :]], axis=0)
            v = jnp.concatenate([v4_ref[prv, r4, rows, :], v4_ref[cur, r4, rows, :]], axis=0)
            o, m, l = block(q4_ref[r4, rows, :], k, v, bias)
            o3_ref[r4, rows, :] = o
            m3_ref[r4, rows, :] = bcast(m)
            l3_ref[r4, rows, :] = l

    def body_merge(sub, carry):
        r0 = pl.multiple_of(sub * B, B)
        for r4 in range(R4):
            nat = pl.ds(sub * (B * R4) + r4, B, stride=R4)
            m1, m2, m3 = m1_ref[nat, :], m2_ref[r4, pl.ds(r0, B), :], m3_ref[r4, pl.ds(r0, B), :]
            mx = jnp.maximum(jnp.maximum(m1, m2), m3)
            w1, w2, w3 = jnp.exp2(m1 - mx), jnp.exp2(m2 - mx), jnp.exp2(m3 - mx)
            num = (o1_ref[nat, :] * w1 + o2_ref[r4, pl.ds(r0, B), :] * w2
                   + o3_ref[r4, pl.ds(r0, B), :] * w3)
            den = (l1_ref[nat, :] * w1 + l2_ref[r4, pl.ds(r0, B), :] * w2
                   + l3_ref[r4, pl.ds(r0, B), :] * w3)
            out_ref[nat, :] = num / den
        return carry

    lax.fori_loop(0, Q // B, body_merge, 0)

    o_ref[...] = (out_ref[...] * _silu(z_ref[...])).astype(o_ref.dtype)
    _side_cast(wf_ref, wb_ref, h * pl.num_programs(1) + c, cast_blocks)


def _attention(proj, rel_bias, q_gain, k_gain, n_heads, z_col, w_stack, layer):
    S = proj.shape[0]
    C, B, Dh = ATTN_CHUNK, ATTN_BLOCK, ATTN_HEAD_DIM
    assert S % C == 0
    H = n_heads
    NC = S // C
    w_in_spec, w_out_spec, cast_blocks = _side_cast_specs(w_stack, layer, H * NC,
                                                          lambda h, c: h * NC + c)
    bkt = jnp.asarray(_attn_bucket_table())
    R4 = DILATIONS[1]
    blk = lambda f: pl.BlockSpec((C, Dh), f)
    nat = pltpu.VMEM((C, Dh), F32)
    mod4 = pltpu.VMEM((R4, C // R4, Dh), F32)
    return pl.pallas_call(
        functools.partial(_attn_body, cast_blocks=cast_blocks),
        grid=(H, NC),
        in_specs=[
            pl.BlockSpec(memory_space=pltpu.SMEM),
            pl.BlockSpec(bkt.shape, lambda h, c: (0, 0, 0)),
            blk(lambda h, c: (c, h)),
            blk(lambda h, c: (c, H + h)),
            blk(lambda h, c: (c, 2 * H + h)),
            blk(lambda h, c: (c, z_col + h)),
            pl.BlockSpec((1, Dh), lambda h, c: (0, 0)),
            pl.BlockSpec((1, Dh), lambda h, c: (0, 0)),
            w_in_spec,
        ],
        out_specs=[blk(lambda h, c: (c, h)), w_out_spec],
        out_shape=[jax.ShapeDtypeStruct((S, H * Dh), BF16),
                   jax.ShapeDtypeStruct(w_stack.shape[1:], BF16)],
        scratch_shapes=[
            pltpu.VMEM((2 * len(DILATIONS), B, 2 * B), F32),
            nat, mod4,
            pltpu.VMEM((2, C, Dh), F32), pltpu.VMEM((2, R4, C // R4, Dh), F32),
            pltpu.VMEM((2, C, Dh), F32), pltpu.VMEM((2, R4, C // R4, Dh), F32),
            nat, nat, nat,
            mod4, mod4, mod4,
            mod4, mod4, mod4,
            nat,
        ],
        compiler_params=_params("arbitrary", "arbitrary"),
        name="dilated_attention",
    )(rel_bias.astype(F32), bkt, proj, proj, proj, proj,
      q_gain.reshape(1, Dh).astype(F32), k_gain.reshape(1, Dh).astype(F32), w_stack)


def _pool_body(u_ref, halo_ref, z_ref, w_ref, sc_ref, o_ref, *, group_dim):
    T = u_ref.shape[0]
    i = pl.program_id(0)
    G = len(POOL_WINDOWS)
    halo_on = jnp.where(i > 0, 1.0, 0.0)
    pos = (i * T + lax.broadcasted_iota(jnp.int32, (T, 1), 0) + 1).astype(F32)
    for g, w in enumerate(POOL_WINDOWS):
        cols = slice(g * group_dim, (g + 1) * group_dim)
        x = u_ref[:, cols]
        e = jnp.concatenate([halo_ref[:, cols] * halo_on, x], axis=0)
        width = 1
        while width < w:
            e = e[width:, :] + e[:-width, :]
            width *= 2
        off = POOL_HALO - (w - 1)
        win = e[off:off + T, :]
        y = win / jnp.minimum(pos, float(w)) - x
        yp = jnp.dot(y.astype(BF16), w_ref[g].astype(BF16), preferred_element_type=F32)
        o_ref[:, cols] = (yp * sc_ref[:, cols] * _silu(z_ref[:, cols])).astype(o_ref.dtype)


def _pool(proj, pool_w, pool_scale, u_col, z_col, tile=256):
    S = proj.shape[0]
    G, Cg, _ = pool_w.shape
    P = G * Cg
    T = min(tile, S)
    assert u_col % P == 0 and z_col % P == 0 and T % POOL_HALO == 0
    return pl.pallas_call(
        functools.partial(_pool_body, group_dim=Cg),
        grid=(S // T,),
        in_specs=[
            pl.BlockSpec((T, P), lambda i: (i, u_col // P)),
            pl.BlockSpec((POOL_HALO, P), lambda i: (jnp.maximum(i * (T // POOL_HALO) - 1, 0), u_col // P)),
            pl.BlockSpec((T, P), lambda i: (i, z_col // P)),
            pl.BlockSpec((G, Cg, Cg), lambda i: (0, 0, 0)),
            pl.BlockSpec((1, P), lambda i: (0, 0)),
        ],
        out_specs=pl.BlockSpec((T, P), lambda i: (i, 0)),
        out_shape=jax.ShapeDtypeStruct((S, P), BF16),
        compiler_params=_params("parallel"),
        name="multiscale_pool",
    )(proj, proj, proj, pool_w.astype(F32), pool_scale.reshape(1, P).astype(F32))


def _front_body(xm_ref, halo_ref, cw_ref, cb_ref, wq_ref, wk_ref, wv_ref, wif_ref, bif_ref,
                xc_ref, q_ref, k_ref, v_ref, g_ref):
    T, TC = xm_ref.shape
    i = pl.program_id(0)
    j = pl.program_id(1)
    xm = xm_ref[...]
    halo = halo_ref[...] * jnp.where(i > 0, 1.0, 0.0)
    e = jnp.concatenate([halo, xm], axis=0)
    conv = cb_ref[...]
    for t in range(CONV_WIDTH):
        off = CONV_HALO - (CONV_WIDTH - 1) + t
        conv = conv + e[off:off + T, :] * cw_ref[t:t + 1, :]
    xc = _silu(conv)
    xc_ref[...] = xc
    xcb = xc.astype(BF16)
    xmb = xm.astype(BF16)
    gates = jnp.zeros(g_ref.shape, F32)
    W = wq_ref.shape[-1]
    for (src, w_ref, o_ref, p) in ((xcb, wq_ref, q_ref, 0), (xcb, wk_ref, k_ref, 1), (xmb, wv_ref, v_ref, 2)):
        for n in range(TC // W):
            cols = slice(n * W, (n + 1) * W)
            y = jnp.dot(src[:, cols], w_ref[n], preferred_element_type=F32)
            yb = y.astype(BF16)
            o_ref[:, cols] = yb
            gates = gates + jnp.dot(yb, wif_ref[p, cols, :], preferred_element_type=F32)

    @pl.when(j == 0)
    def _():
        g_ref[...] = bif_ref[...] + gates

    @pl.when(j > 0)
    def _():
        g_ref[...] += gates


def _block_diag_dense(w, width):
    nb, bs, _ = w.shape
    per = width // bs
    wg = w.reshape(nb // per, per, bs, bs)
    eye = jnp.eye(per, dtype=w.dtype)
    dense = jnp.einsum('gpcd,pq->gpcqd', wg, eye)
    return dense.reshape(nb // per, width, width).astype(BF16)


def _mlstm_front(up, conv_w, conv_b, wq, wk, wv, w_if, b_if, tile=512, tcol=1024):
    S = up.shape[0]
    E = conv_w.shape[1]
    T, TC = min(tile, S), min(tcol, E)
    NG = w_if.shape[1]
    W = min(QKV_TILE, TC)
    nt = TC // W
    wdense = [_block_diag_dense(w, W) for w in (wq, wk, wv)]
    wif = w_if.reshape(3, E, NG).astype(BF16)
    row = lambda: pl.BlockSpec((T, TC), lambda i, j: (i, j))
    return pl.pallas_call(
        _front_body,
        grid=(S // T, E // TC),
        in_specs=[
            row(),
            pl.BlockSpec((CONV_HALO, TC), lambda i, j: (jnp.maximum(i * (T // CONV_HALO) - 1, 0), j)),
            pl.BlockSpec((CONV_WIDTH, TC), lambda i, j: (0, j)),
            pl.BlockSpec((1, TC), lambda i, j: (0, j)),
            pl.BlockSpec((nt, W, W), lambda i, j: (j, 0, 0)),
            pl.BlockSpec((nt, W, W), lambda i, j: (j, 0, 0)),
            pl.BlockSpec((nt, W, W), lambda i, j: (j, 0, 0)),
            pl.BlockSpec((3, TC, NG), lambda i, j: (0, j, 0)),
            pl.BlockSpec((1, NG), lambda i, j: (0, 0)),
        ],
        out_specs=[row(), row(), row(), row(), pl.BlockSpec((T, NG), lambda i, j: (i, 0))],
        out_shape=[jax.ShapeDtypeStruct((S, E), F32)] + [jax.ShapeDtypeStruct((S, E), BF16)] * 3
        + [jax.ShapeDtypeStruct((S, NG), F32)],
        compiler_params=_params("parallel", "arbitrary"),
        name="mlstm_front",
    )(up, up, conv_w.astype(F32), conv_b.reshape(1, E).astype(F32), *wdense, wif,
      b_if.reshape(1, NG).astype(F32))


def _mlstm_body(q_ref, k_ref, v_ref, ig_ref, fg_ref, op_ref, xc_ref, z_ref, gn_ref, sk_ref,
                wf_ref, o_ref, wb_ref, c_ref, cb_ref, n_ref, m_ref, *, cast_blocks):
    L = q_ref.shape[0]
    G, DK, _ = c_ref.shape
    c = pl.program_id(1)
    _side_cast(wf_ref, wb_ref, pl.program_id(0) * pl.num_programs(1) + c, cast_blocks)

    @pl.when(c == 0)
    def _():
        c_ref[...] = jnp.zeros_like(c_ref)
        cb_ref[...] = jnp.zeros_like(cb_ref)
        n_ref[...] = jnp.zeros_like(n_ref)
        m_ref[...] = jnp.full(m_ref.shape, -1e30, F32)

    ri = lax.broadcasted_iota(jnp.int32, (L, L), 0)
    cj = lax.broadcasted_iota(jnp.int32, (L, L), 1)
    lane8 = lax.broadcasted_iota(jnp.int32, (8, L), 1)
    scale = DK ** -0.5

    for hh in range(G):
        cols = slice(hh * DK, (hh + 1) * DK)
        i_row = ig_ref[hh, 0]
        f_row = fg_ref[hh, 0]
        lf_row = jnp.minimum(f_row, 0.0) - jnp.log1p(jnp.exp(-jnp.abs(f_row)))
        b8 = jnp.broadcast_to(lf_row, (8, L))
        sh = 1
        while sh < L:
            b8 = b8 + jnp.where(lane8 >= sh, pltpu.roll(b8, sh, 1), 0.0)
            sh *= 2
        b_row = b8[0:1, :]
        stacked = jnp.where(ri == 0, jnp.broadcast_to(b_row, (L, L)),
                            jnp.where(ri == 1, jnp.broadcast_to(i_row, (L, L)), 0.0))
        stacked_t = stacked.T
        b_col = stacked_t[:, 0:1]
        i_col = stacked_t[:, 1:2]

        m_prev = m_ref[hh, 0:1, 0:1]
        log_d = jnp.where(cj <= ri, b_col - b_row + i_row, -jnp.inf)
        log_inter = b_col + m_prev
        m_t = jnp.maximum(jnp.max(log_d, axis=-1, keepdims=True), log_inter)
        dmat = jnp.exp(log_d - m_t) * scale
        g = jnp.exp(log_inter - m_t)

        q = q_ref[:, cols]
        k = k_ref[:, cols]
        v = v_ref[:, cols]
        s = lax.dot_general(q, k, (((1,), (1,)), ((), ())), preferred_element_type=F32) * dmat
        inter = jnp.dot(q, cb_ref[hh], preferred_element_type=F32)
        num = jnp.dot(s.astype(BF16), v, preferred_element_type=F32) + g * inter
        qn = jnp.sum(q.astype(F32) * n_ref[hh], axis=-1, keepdims=True)
        den = jnp.sum(s, axis=-1, keepdims=True) + g * qn
        hc = num / jnp.maximum(jnp.abs(den), jnp.exp(-m_t))

        m_new = m_t[L - 1:L, :]
        b_last = b_col[L - 1:L, :]
        decay = jnp.exp(b_last + m_prev - m_new)
        w_col = jnp.exp(b_last - b_col + i_col - m_new) * scale
        vw = (v.astype(F32) * w_col).astype(BF16)
        upd = lax.dot_general(k, vw, (((0,), (0,)), ((), ())), preferred_element_type=F32)
        c_new = upd + c_ref[hh] * decay
        c_ref[hh] = c_new
        cb_ref[hh] = c_new.astype(BF16)
        n_ref[hh] = n_ref[hh] * decay + jnp.sum(k.astype(F32) * w_col, axis=0, keepdims=True)
        m_ref[hh] = jnp.broadcast_to(m_new, m_ref.shape[1:])

        mu = jnp.mean(hc, axis=-1, keepdims=True)
        ctr = hc - mu
        var = jnp.mean(ctr * ctr, axis=-1, keepdims=True)
        hn = ctr * lax.rsqrt(var + NORM_EPS)
        cell = (1.0 / (1.0 + jnp.exp(-op_ref[:, cols]))) * (hn * gn_ref[:, cols])
        o_ref[:, cols] = ((cell + sk_ref[:, cols] * xc_ref[:, cols])
                          * _silu(z_ref[:, cols])).astype(o_ref.dtype)


def _mlstm(q, k, v, gates, up, xc, gn, skip, w_stack, layer):
    S, E = q.shape
    H, L, G = MLSTM_HEADS, MLSTM_CHUNK, MLSTM_GROUP
    DH = E // H
    NCH = S // L
    HG = H // G
    w_in_spec, w_out_spec, cast_blocks = _side_cast_specs(w_stack, layer, HG * NCH,
                                                          lambda h, c: h * NCH + c)
    gt = gates.T.reshape(2 * H, NCH, 1, L)
    blk = lambda col0: pl.BlockSpec((L, G * DH), lambda h, c: (c, col0 + h))
    vec = pl.BlockSpec((1, G * DH), lambda h, c: (0, h))
    return pl.pallas_call(
        functools.partial(_mlstm_body, cast_blocks=cast_blocks),
        grid=(HG, NCH),
        in_specs=[
            blk(0), blk(0), blk(0),
            pl.BlockSpec((G, 1, 1, L), lambda h, c: (h, c, 0, 0)),
            pl.BlockSpec((G, 1, 1, L), lambda h, c: (HG + h, c, 0, 0)),
            blk(2 * HG), blk(0), blk(HG),
            vec, vec,
            w_in_spec,
        ],
        out_specs=[blk(0), w_out_spec],
        out_shape=[jax.ShapeDtypeStruct((S, E), BF16), jax.ShapeDtypeStruct(w_stack.shape[1:], BF16)],
        scratch_shapes=[
            pltpu.VMEM((G, DH, DH), F32),
            pltpu.VMEM((G, DH, DH), BF16),
            pltpu.VMEM((G, 1, DH), F32),
            pltpu.VMEM((G, 8, LANES), F32),
        ],
        compiler_params=_params("arbitrary", "arbitrary"),
        name="mlstm_chunkwise",
    )(q, k, v, gt, gt, up, xc, up, gn.reshape(1, E).astype(F32), skip.reshape(1, E).astype(F32),
      w_stack)


def _even_layer(h, xg, r, j, rel_bias, w_in, q_gain, k_gain, pool_w, pool_scale, w_out, next_gain):
    mix = w_out.shape[1]
    pool_width = pool_w.shape[0] * pool_w.shape[1]
    attn_width = mix - pool_width
    n_heads = attn_width // ATTN_HEAD_DIM
    u_col = 3 * attn_width
    z_col = u_col + pool_width
    proj = _matmul([xg], w_in, j, row_scale=r, tm=IN_PROJ_ROWS)
    attn, w_out_bf16 = _attention(proj, rel_bias, q_gain, k_gain, n_heads,
                                  z_col // ATTN_HEAD_DIM, w_out, j)
    pool = _pool(proj, pool_w, pool_scale, u_col, z_col + attn_width)
    return _matmul([attn, pool], w_out_bf16, j, residual=h, next_gain=next_gain, tn=OUT_PROJ_COLS)


def _odd_layer(h, xg, r, j, w_up, conv_w, conv_b, wq, wk, wv, w_if, b_if, gn, skip, w_down,
               next_gain):
    up = _matmul([xg], w_up, j, row_scale=r, tm=IN_PROJ_ROWS)
    xc, q, k, v, gates = _mlstm_front(up, conv_w, conv_b, wq, wk, wv, w_if, b_if)
    out, w_down_bf16 = _mlstm(q, k, v, gates, up, xc, gn, skip, w_down, j)
    return _matmul([out], w_down_bf16, j, residual=h, next_gain=next_gain, tn=OUT_PROJ_COLS)


def kernel(x, rel_bias, e_norm, e_w_in, e_q_gain, e_k_gain, e_pool_w, e_pool_scale, e_w_out,
           o_norm, o_w_up, o_conv_w, o_conv_b, o_wq, o_wk, o_wv, o_w_if, o_b_if, o_gn, o_skip,
           o_w_down):
    B, S, D = x.shape
    depth = e_norm.shape[0] + o_norm.shape[0]
    outs = []
    for b in range(B):
        h = x[b]
        gains = [(e_norm if layer % 2 == 0 else o_norm)[layer // 2] for layer in range(depth)]
        xg, r = _norm_factors(h, gains[0])
        for layer in range(depth):
            j = layer // 2
            next_gain = gains[layer + 1] if layer + 1 < depth else None
            if layer % 2 == 0:
                res = _even_layer(h, xg, r, j, rel_bias, e_w_in, e_q_gain[j], e_k_gain[j],
                                  e_pool_w[j], e_pool_scale[j], e_w_out, next_gain)
            else:
                res = _odd_layer(h, xg, r, j, o_w_up, o_conv_w[j], o_conv_b[j], o_wq[j], o_wk[j],
                                 o_wv[j], o_w_if[j], o_b_if[j], o_gn[j], o_skip[j], o_w_down,
                                 next_gain)
            h, xg, r = res if next_gain is not None else (res, None, None)
        outs.append(h)
    return jnp.stack(outs)
```

```python
import functools

import numpy as np
import jax
import jax.numpy as jnp
from jax import lax
from jax.experimental import pallas as pl
from jax.experimental.pallas import tpu as pltpu

F32 = jnp.float32
BF16 = jnp.bfloat16

NORM_EPS = 1e-6
IN_PROJ_ROWS = 2048
OUT_PROJ_COLS = 512
LOG2E = 1.4426950408889634
LANES = 128
ATTN_HEAD_DIM = 128
ATTN_BLOCK = 128
DILATIONS = (1, 4, 16)
ATTN_CHUNK = ATTN_BLOCK * DILATIONS[-1]
POOL_WINDOWS = (2, 4, 8, 16)
POOL_HALO = 16
REL_BUCKETS = 32
REL_MAX_DIST = 2048
MLSTM_HEADS = 8
MLSTM_CHUNK = 256
MLSTM_GROUP = 2
CONV_WIDTH = 4
CONV_HALO = 8
QKV_TILE = 128
VMEM_LIMIT = 56 * 1024 * 1024


def _params(*sem):
    return pltpu.CompilerParams(dimension_semantics=sem, vmem_limit_bytes=VMEM_LIMIT)


def _silu(x):
    return x * (1.0 / (1.0 + jnp.exp(-x)))


def _mm_body(*refs, n_a, has_res, has_scale, has_next, n_cols):
    a_refs = refs[:n_a]
    w_refs = refs[n_a:2 * n_a]
    p = 2 * n_a
    res_ref = refs[p] if has_res else None
    p += int(has_res)
    scale_ref = refs[p] if has_scale else None
    p += int(has_scale)
    gain_ref = refs[p] if has_next else None
    p += int(has_next)
    o_ref = refs[p]
    acc = None
    for a_ref, w_ref in zip(a_refs, w_refs):
        w = w_ref[...]
        d = jnp.dot(a_ref[...], w if w.dtype == BF16 else w.astype(BF16),
                    preferred_element_type=F32)
        acc = d if acc is None else acc + d
    if has_scale:
        acc = acc * scale_ref[:, 0:1]
    if has_res:
        acc = res_ref[...] + acc
    o_ref[...] = acc.astype(o_ref.dtype)
    if has_next:
        xg_ref, r_ref, ssq_ref = refs[p + 1], refs[p + 2], refs[p + 3]
        j = pl.program_id(1)
        xg_ref[...] = (acc * gain_ref[...]).astype(BF16)
        part = jnp.broadcast_to(jnp.sum(acc * acc, axis=-1, keepdims=True), ssq_ref.shape)

        @pl.when(j == 0)
        def _():
            ssq_ref[...] = part

        @pl.when(j > 0)
        def _():
            ssq_ref[...] += part

        @pl.when(j == pl.num_programs(1) - 1)
        def _():
            r_ref[...] = lax.rsqrt(ssq_ref[...] * (1.0 / n_cols) + NORM_EPS)


def _matmul(a_list, w_stack, layer, residual=None, row_scale=None, next_gain=None,
            out_dtype=F32, tm=1024, tn=512):
    M = a_list[0].shape[0]
    K, N = w_stack.shape[-2:]
    tm, tn = min(tm, M), min(tn, N)
    assert M % tm == 0 and N % tn == 0 and sum(a.shape[1] for a in a_list) == K
    in_specs, w_specs, row0 = [], [], 0
    for a in a_list:
        kp = a.shape[1]
        assert row0 % kp == 0
        in_specs.append(pl.BlockSpec((tm, kp), lambda i, j: (i, 0), pipeline_mode=pl.Buffered(1)))
        if w_stack.ndim == 3:
            w_specs.append(pl.BlockSpec((None, kp, tn), lambda i, j, rb=row0 // kp: (layer, rb, j)))
        else:
            w_specs.append(pl.BlockSpec((kp, tn), lambda i, j, rb=row0 // kp: (rb, j)))
        row0 += kp
    in_specs += w_specs
    args = list(a_list) + [w_stack] * len(a_list)
    tile = pl.BlockSpec((tm, tn), lambda i, j: (i, j))
    rows = pl.BlockSpec((tm, LANES), lambda i, j: (i, 0))
    if residual is not None:
        in_specs.append(tile)
        args.append(residual)
    if row_scale is not None:
        in_specs.append(rows)
        args.append(row_scale)
    out_specs, out_shape, scratch = tile, jax.ShapeDtypeStruct((M, N), out_dtype), []
    if next_gain is not None:
        in_specs.append(pl.BlockSpec((1, tn), lambda i, j: (0, j)))
        args.append(next_gain.reshape(1, N).astype(F32))
        out_specs = [tile, tile, rows]
        out_shape = [out_shape, jax.ShapeDtypeStruct((M, N), BF16), jax.ShapeDtypeStruct((M, LANES), F32)]
        scratch = [pltpu.VMEM((tm, LANES), F32)]
    return pl.pallas_call(
        functools.partial(_mm_body, n_a=len(a_list), has_res=residual is not None,
                          has_scale=row_scale is not None, has_next=next_gain is not None, n_cols=N),
        grid=(M // tm, N // tn),
        in_specs=in_specs,
        out_specs=out_specs,
        out_shape=out_shape,
        scratch_shapes=scratch,
        compiler_params=_params("arbitrary", "arbitrary"),
        name="matmul",
    )(*args)


def _side_cast_specs(w_stack, layer, n_steps, step_of):
    K, N = w_stack.shape[1:]
    n_blocks = 1 << (n_steps.bit_length() - 1)
    while K % n_blocks or (K // n_blocks) % 16:
        n_blocks //= 2
    rb = K // n_blocks
    idx = lambda *g: jnp.minimum(step_of(*g), n_blocks - 1)
    return (pl.BlockSpec((None, rb, N), lambda *g: (layer, idx(*g), 0)),
            pl.BlockSpec((rb, N), lambda *g: (idx(*g), 0)), n_blocks)


def _side_cast(wf_ref, wb_ref, step, n_blocks):
    @pl.when(step < n_blocks)
    def _():
        wb_ref[...] = wf_ref[...].astype(BF16)


def _norm_factors_body(x_ref, g_ref, xg_ref, r_ref):
    x = x_ref[...]
    xg_ref[...] = (x * g_ref[...]).astype(BF16)
    ms = jnp.mean(x * x, axis=-1, keepdims=True)
    r_ref[...] = jnp.broadcast_to(lax.rsqrt(ms + NORM_EPS), r_ref.shape)


def _norm_factors(x, gain, tm=256):
    M, D = x.shape
    tm = min(tm, M)
    return pl.pallas_call(
        _norm_factors_body,
        grid=(M // tm,),
        in_specs=[pl.BlockSpec((tm, D), lambda i: (i, 0)),
                  pl.BlockSpec((1, D), lambda i: (0, 0))],
        out_specs=[pl.BlockSpec((tm, D), lambda i: (i, 0)),
                   pl.BlockSpec((tm, LANES), lambda i: (i, 0))],
        out_shape=[jax.ShapeDtypeStruct((M, D), BF16), jax.ShapeDtypeStruct((M, LANES), F32)],
        compiler_params=_params("parallel"),
        name="norm_factors",
    )(x, gain.reshape(1, D).astype(F32))


def _t5_bucket(dist):
    max_exact = REL_BUCKETS // 2
    safe = np.maximum(dist, 1).astype(np.float32)
    large = max_exact + (np.log(safe / max_exact) / np.log(REL_MAX_DIST / max_exact)
                         * (REL_BUCKETS - max_exact)).astype(np.int32)
    large = np.minimum(large, REL_BUCKETS - 1)
    return np.where(dist < max_exact, dist, large).astype(np.int32)


def _attn_bucket_table():
    B = ATTN_BLOCK
    rel = B - np.arange(2 * B)
    band = (rel >= 0) & (rel <= B)
    rows = [np.where(band, _t5_bucket(np.clip(rel, 0, None) * d), -1) for d in DILATIONS]
    return np.broadcast_to(np.stack(rows)[:, None, :], (len(DILATIONS), 8, 2 * B)).astype(np.int32)


def _attn_body(relb_ref, bkt_ref, q_ref, k_ref, v_ref, z_ref, qg_ref, kg_ref, wf_ref,
               o_ref, wb_ref, bias_ref, qn_ref, q4_ref, kn_ref, k4_ref, vn_ref, v4_ref,
               o1_ref, m1_ref, l1_ref, o2_ref, m2_ref, l2_ref, o3_ref, m3_ref, l3_ref, out_ref,
               *, cast_blocks):
    B = ATTN_BLOCK
    C = ATTN_CHUNK
    R4 = DILATIONS[1]
    Q = C // R4
    h = pl.program_id(0)
    c = pl.program_id(1)
    n_dil = len(DILATIONS)
    cur = c % 2
    prv = 1 - cur

    @pl.when(c == 0)
    def _():
        col = lax.broadcasted_iota(jnp.int32, (B, 2 * B), 1)
        for t in range(n_dil):
            bkt = bkt_ref[t]
            row = jnp.full(bkt.shape, -jnp.inf, F32)
            for b in range(REL_BUCKETS):
                row = jnp.where(bkt == b, relb_ref[b, h] * LOG2E, row)
            bias = pltpu.roll(jnp.broadcast_to(row[0:1, :], (B, 2 * B)), 0, 1,
                              stride=1, stride_axis=0)
            bias_ref[t] = bias
            bias_ref[t + n_dil] = jnp.where(col >= B, bias, -jnp.inf)

    @pl.when(jnp.logical_and(h == 0, c == 0))
    def _():
        kn_ref[1] = jnp.zeros(kn_ref.shape[1:], F32)
        vn_ref[1] = jnp.zeros(vn_ref.shape[1:], F32)
        k4_ref[1] = jnp.zeros(k4_ref.shape[1:], F32)
        v4_ref[1] = jnp.zeros(v4_ref.shape[1:], F32)

    def _norm(x, g):
        ms = jnp.mean(x * x, axis=-1, keepdims=True)
        return x * lax.rsqrt(ms + NORM_EPS) * g

    qn_ref[...] = _norm(q_ref[...], qg_ref[...]) * (ATTN_HEAD_DIM ** -0.5 * LOG2E)
    kn_ref[cur] = _norm(k_ref[...], kg_ref[...])
    vn_ref[cur] = v_ref[...]
    for r4 in range(R4):
        rows = pl.ds(r4, Q, stride=R4)
        q4_ref[r4] = qn_ref[rows, :]
        k4_ref[cur, r4] = kn_ref[cur, rows, :]
        v4_ref[cur, r4] = vn_ref[cur, rows, :]

    first_chunk = jnp.where(c == 0, 1, 0)

    def block(q, k, v, bias):
        s = lax.dot_general(q.astype(BF16), k.astype(BF16), (((1,), (1,)), ((), ())),
                            preferred_element_type=F32) + bias
        m = jnp.max(s, axis=-1, keepdims=True)
        p = jnp.exp2(s - m).astype(BF16)
        v_aug = jnp.concatenate([v.astype(BF16), jnp.ones((2 * B, LANES), BF16)], axis=1)
        o_aug = jnp.dot(p, v_aug, preferred_element_type=F32)
        return o_aug[:, :ATTN_HEAD_DIM], m, o_aug[:, ATTN_HEAD_DIM:]

    def bcast(x):
        return jnp.broadcast_to(x, (B, LANES))

    for b in range(C // B):
        rb = slice(b * B, (b + 1) * B)
        if b == 0:
            ka, va = kn_ref[prv, C - B:C, :], vn_ref[prv, C - B:C, :]
            bias = bias_ref[n_dil * first_chunk]
        else:
            ka, va = kn_ref[cur, (b - 1) * B:b * B, :], vn_ref[cur, (b - 1) * B:b * B, :]
            bias = bias_ref[0]
        k = jnp.concatenate([ka, kn_ref[cur, rb, :]], axis=0)
        v = jnp.concatenate([va, vn_ref[cur, rb, :]], axis=0)
        o, m, l = block(qn_ref[rb, :], k, v, bias)
        o1_ref[rb, :] = o
        m1_ref[rb, :] = bcast(m)
        l1_ref[rb, :] = l

    for sub in range(Q // B):
        rb = slice(sub * B, (sub + 1) * B)
        bias = bias_ref[1 + n_dil * first_chunk] if sub == 0 else bias_ref[1]
        for r4 in range(R4):
            if sub == 0:
                ka, va = k4_ref[prv, r4, Q - B:Q, :], v4_ref[prv, r4, Q - B:Q, :]
            else:
                ra = slice((sub - 1) * B, sub * B)
                ka, va = k4_ref[cur, r4, ra, :], v4_ref[cur, r4, ra, :]
            k = jnp.concatenate([ka, k4_ref[cur, r4, rb, :]], axis=0)
            v = jnp.concatenate([va, v4_ref[cur, r4, rb, :]], axis=0)
            o, m, l = block(q4_ref[r4, rb, :], k, v, bias)
            o2_ref[r4, rb, :] = o
            m2_ref[r4, rb, :] = bcast(m)
            l2_ref[r4, rb, :] = l

    bias = bias_ref[2 + n_dil * first_chunk]
    for o4 in range(Q // B):
        rows = pl.ds(o4, B, stride=R4)
        for r4 in range(R4):
            k = jnp.concatenate([k4_ref[prv, r4, rows, :], k4_ref[cur, r4, rows, :]], axis=0)
            v = jnp.concatenate([v4_ref[prv, r4, rows, :], v4_ref[cur, r4, rows, :]], axis=0)
            o, m, l = block(q4_ref[r4, rows, :], k, v, bias)
            o3_ref[r4, rows, :] = o
            m3_ref[r4, rows, :] = bcast(m)
            l3_ref[r4, rows, :] = l

    def body_merge(sub, carry):
        r0 = pl.multiple_of(sub * B, B)
        for r4 in range(R4):
            nat = pl.ds(sub * (B * R4) + r4, B, stride=R4)
            m1, m2, m3 = m1_ref[nat, :], m2_ref[r4, pl.ds(r0, B), :], m3_ref[r4, pl.ds(r0, B), :]
            mx = jnp.maximum(jnp.maximum(m1, m2), m3)
            w1, w2, w3 = jnp.exp2(m1 - mx), jnp.exp2(m2 - mx), jnp.exp2(m3 - mx)
            num = (o1_ref[nat, :] * w1 + o2_ref[r4, pl.ds(r0, B), :] * w2
                   + o3_ref[r4, pl.ds(r0, B), :] * w3)
            den = (l1_ref[nat, :] * w1 + l2_ref[r4, pl.ds(r0, B), :] * w2
                   + l3_ref[r4, pl.ds(r0, B), :] * w3)
            out_ref[nat, :] = num / den
        return carry

    lax.fori_loop(0, Q // B, body_merge, 0)

    o_ref[...] = (out_ref[...] * _silu(z_ref[...])).astype(o_ref.dtype)
    _side_cast(wf_ref, wb_ref, h * pl.num_programs(1) + c, cast_blocks)


def _attention(proj, rel_bias, q_gain, k_gain, n_heads, z_col, w_stack, layer):
    S = proj.shape[0]
    C, B, Dh = ATTN_CHUNK, ATTN_BLOCK, ATTN_HEAD_DIM
    assert S % C == 0
    H = n_heads
    NC = S // C
    w_in_spec, w_out_spec, cast_blocks = _side_cast_specs(w_stack, layer, H * NC,
                                                          lambda h, c: h * NC + c)
    bkt = jnp.asarray(_attn_bucket_table())
    R4 = DILATIONS[1]
    blk = lambda f: pl.BlockSpec((C, Dh), f)
    nat = pltpu.VMEM((C, Dh), F32)
    mod4 = pltpu.VMEM((R4, C // R4, Dh), F32)
    return pl.pallas_call(
        functools.partial(_attn_body, cast_blocks=cast_blocks),
        grid=(H, NC),
        in_specs=[
            pl.BlockSpec(memory_space=pltpu.SMEM),
            pl.BlockSpec(bkt.shape, lambda h, c: (0, 0, 0)),
            blk(lambda h, c: (c, h)),
            blk(lambda h, c: (c, H + h)),
            blk(lambda h, c: (c, 2 * H + h)),
            blk(lambda h, c: (c, z_col + h)),
            pl.BlockSpec((1, Dh), lambda h, c: (0, 0)),
            pl.BlockSpec((1, Dh), lambda h, c: (0, 0)),
            w_in_spec,
        ],
        out_specs=[blk(lambda h, c: (c, h)), w_out_spec],
        out_shape=[jax.ShapeDtypeStruct((S, H * Dh), BF16),
                   jax.ShapeDtypeStruct(w_stack.shape[1:], BF16)],
        scratch_shapes=[
            pltpu.VMEM((2 * len(DILATIONS), B, 2 * B), F32),
            nat, mod4,
            pltpu.VMEM((2, C, Dh), F32), pltpu.VMEM((2, R4, C // R4, Dh), F32),
            pltpu.VMEM((2, C, Dh), F32), pltpu.VMEM((2, R4, C // R4, Dh), F32),
            nat, nat, nat,
            mod4, mod4, mod4,
            mod4, mod4, mod4,
            nat,
        ],
        compiler_params=_params("arbitrary", "arbitrary"),
        name="dilated_attention",
    )(rel_bias.astype(F32), bkt, proj, proj, proj, proj,
      q_gain.reshape(1, Dh).astype(F32), k_gain.reshape(1, Dh).astype(F32), w_stack)


def _pool_body(u_ref, halo_ref, z_ref, w_ref, sc_ref, o_ref, *, group_dim):
    T = u_ref.shape[0]
    i = pl.program_id(0)
    G = len(POOL_WINDOWS)
    halo_on = jnp.where(i > 0, 1.0, 0.0)
    pos = (i * T + lax.broadcasted_iota(jnp.int32, (T, 1), 0) + 1).astype(F32)
    for g, w in enumerate(POOL_WINDOWS):
        cols = slice(g * group_dim, (g + 1) * group_dim)
        x = u_ref[:, cols]
        e = jnp.concatenate([halo_ref[:, cols] * halo_on, x], axis=0)
        width = 1
        while width < w:
            e = e[width:, :] + e[:-width, :]
            width *= 2
        off = POOL_HALO - (w - 1)
        win = e[off:off + T, :]
        y = win / jnp.minimum(pos, float(w)) - x
        yp = jnp.dot(y.astype(BF16), w_ref[g].astype(BF16), preferred_element_type=F32)
        o_ref[:, cols] = (yp * sc_ref[:, cols] * _silu(z_ref[:, cols])).astype(o_ref.dtype)


def _pool(proj, pool_w, pool_scale, u_col, z_col, tile=256):
    S = proj.shape[0]
    G, Cg, _ = pool_w.shape
    P = G * Cg
    T = min(tile, S)
    assert u_col % P == 0 and z_col % P == 0 and T % POOL_HALO == 0
    return pl.pallas_call(
        functools.partial(_pool_body, group_dim=Cg),
        grid=(S // T,),
        in_specs=[
            pl.BlockSpec((T, P), lambda i: (i, u_col // P)),
            pl.BlockSpec((POOL_HALO, P), lambda i: (jnp.maximum(i * (T // POOL_HALO) - 1, 0), u_col // P)),
            pl.BlockSpec((T, P), lambda i: (i, z_col // P)),
            pl.BlockSpec((G, Cg, Cg), lambda i: (0, 0, 0)),
            pl.BlockSpec((1, P), lambda i: (0, 0)),
        ],
        out_specs=pl.BlockSpec((T, P), lambda i: (i, 0)),
        out_shape=jax.ShapeDtypeStruct((S, P), BF16),
        compiler_params=_params("parallel"),
        name="multiscale_pool",
    )(proj, proj, proj, pool_w.astype(F32), pool_scale.reshape(1, P).astype(F32))


def _front_body(xm_ref, halo_ref, cw_ref, cb_ref, wq_ref, wk_ref, wv_ref, wif_ref, bif_ref,
                xc_ref, q_ref, k_ref, v_ref, g_ref):
    T, TC = xm_ref.shape
    i = pl.program_id(0)
    j = pl.program_id(1)
    xm = xm_ref[...]
    halo = halo_ref[...] * jnp.where(i > 0, 1.0, 0.0)
    e = jnp.concatenate([halo, xm], axis=0)
    conv = cb_ref[...]
    for t in reversed(range(CONV_WIDTH)):
        off = CONV_HALO - (CONV_WIDTH - 1) + t
        conv = conv + e[off:off + T, :] * cw_ref[t:t + 1, :]
    xc = _silu(conv)
    xc_ref[...] = xc
    xcb = xc.astype(BF16)
    xmb = xm.astype(BF16)
    gates = jnp.zeros(g_ref.shape, F32)
    W = wq_ref.shape[-1]
    for (src, w_ref, o_ref, p) in ((xcb, wq_ref, q_ref, 0), (xcb, wk_ref, k_ref, 1), (xmb, wv_ref, v_ref, 2)):
        for n in range(TC // W):
            cols = slice(n * W, (n + 1) * W)
            y = jnp.dot(src[:, cols], w_ref[n], preferred_element_type=F32)
            yb = y.astype(BF16)
            o_ref[:, cols] = yb
            gates = gates + jnp.dot(yb, wif_ref[p, cols, :], preferred_element_type=F32)

    @pl.when(j == 0)
    def _():
        g_ref[...] = bif_ref[...] + gates

    @pl.when(j > 0)
    def _():
        g_ref[...] += gates


def _block_diag_dense(w, width):
    nb, bs, _ = w.shape
    per = width // bs
    wg = w.reshape(nb // per, per, bs, bs)
    eye = jnp.eye(per, dtype=w.dtype)
    dense = jnp.einsum('gpcd,pq->gpcqd', wg, eye)
    return dense.reshape(nb // per, width, width).astype(BF16)


def _mlstm_front(up, conv_w, conv_b, wq, wk, wv, w_if, b_if, tile=512, tcol=1024):
    S = up.shape[0]
    E = conv_w.shape[1]
    T, TC = min(tile, S), min(tcol, E)
    NG = w_if.shape[1]
    W = min(QKV_TILE, TC)
    nt = TC // W
    wdense = [_block_diag_dense(w, W) for w in (wq, wk, wv)]
    wif = w_if.reshape(3, E, NG).astype(BF16)
    row = lambda: pl.BlockSpec((T, TC), lambda i, j: (i, j))
    return pl.pallas_call(
        _front_body,
        grid=(S // T, E // TC),
        in_specs=[
            row(),
            pl.BlockSpec((CONV_HALO, TC), lambda i, j: (jnp.maximum(i * (T // CONV_HALO) - 1, 0), j)),
            pl.BlockSpec((CONV_WIDTH, TC), lambda i, j: (0, j)),
            pl.BlockSpec((1, TC), lambda i, j: (0, j)),
            pl.BlockSpec((nt, W, W), lambda i, j: (j, 0, 0)),
            pl.BlockSpec((nt, W, W), lambda i, j: (j, 0, 0)),
            pl.BlockSpec((nt, W, W), lambda i, j: (j, 0, 0)),
            pl.BlockSpec((3, TC, NG), lambda i, j: (0, j, 0)),
            pl.BlockSpec((1, NG), lambda i, j: (0, 0)),
        ],
        out_specs=[row(), row(), row(), row(), pl.BlockSpec((T, NG), lambda i, j: (i, 0))],
        out_shape=[jax.ShapeDtypeStruct((S, E), F32)] + [jax.ShapeDtypeStruct((S, E), BF16)] * 3
        + [jax.ShapeDtypeStruct((S, NG), F32)],
        compiler_params=_params("parallel", "arbitrary"),
        name="mlstm_front",
    )(up, up, conv_w.astype(F32), conv_b.reshape(1, E).astype(F32), *wdense, wif,
      b_if.reshape(1, NG).astype(F32))


def _mlstm_body(q_ref, k_ref, v_ref, ig_ref, fg_ref, op_ref, xc_ref, z_ref, gn_ref, sk_ref,
                wf_ref, o_ref, wb_ref, c_ref, cb_ref, n_ref, m_ref, *, cast_blocks):
    L = q_ref.shape[0]
    G, DK, _ = c_ref.shape
    c = pl.program_id(1)
    _side_cast(wf_ref, wb_ref, pl.program_id(0) * pl.num_programs(1) + c, cast_blocks)

    @pl.when(c == 0)
    def _():
        c_ref[...] = jnp.zeros_like(c_ref)
        cb_ref[...] = jnp.zeros_like(cb_ref)
        n_ref[...] = jnp.zeros_like(n_ref)
        m_ref[...] = jnp.full(m_ref.shape, -1e30, F32)

    ri = lax.broadcasted_iota(jnp.int32, (L, L), 0)
    cj = lax.broadcasted_iota(jnp.int32, (L, L), 1)
    lane8 = lax.broadcasted_iota(jnp.int32, (8, L), 1)
    scale = DK ** -0.5

    for hh in range(G):
        cols = slice(hh * DK, (hh + 1) * DK)
        i_row = ig_ref[hh, 0]
        f_row = fg_ref[hh, 0]
        lf_row = jnp.minimum(f_row, 0.0) - jnp.log1p(jnp.exp(-jnp.abs(f_row)))
        b8 = jnp.broadcast_to(lf_row, (8, L))
        sh = 1
        while sh < L:
            b8 = b8 + jnp.where(lane8 >= sh, pltpu.roll(b8, sh, 1), 0.0)
            sh *= 2
        b_row = b8[0:1, :]
        stacked = jnp.where(ri == 0, jnp.broadcast_to(b_row, (L, L)),
                            jnp.where(ri == 1, jnp.broadcast_to(i_row, (L, L)), 0.0))
        stacked_t = stacked.T
        b_col = stacked_t[:, 0:1]
        i_col = stacked_t[:, 1:2]

        m_prev = m_ref[hh, 0:1, 0:1]
        log_d = jnp.where(cj <= ri, b_col - b_row + i_row, -jnp.inf)
        log_inter = b_col + m_prev
        m_t = jnp.maximum(jnp.max(log_d, axis=-1, keepdims=True), log_inter)
        dmat = jnp.exp(log_d - m_t) * scale
        g = jnp.exp(log_inter - m_t)

        q = q_ref[:, cols]
        k = k_ref[:, cols]
        v = v_ref[:, cols]
        s = lax.dot_general(q, k, (((1,), (1,)), ((), ())), preferred_element_type=F32) * dmat
        inter = jnp.dot(q, cb_ref[hh], preferred_element_type=F32)
        num = jnp.dot(s.astype(BF16), v, preferred_element_type=F32) + g * inter
        qn = jnp.sum(q.astype(F32) * n_ref[hh], axis=-1, keepdims=True)
        den = jnp.sum(s, axis=-1, keepdims=True) + g * qn
        hc = num / jnp.maximum(jnp.abs(den), jnp.exp(-m_t))

        m_new = m_t[L - 1:L, :]
        b_last = b_col[L - 1:L, :]
        decay = jnp.exp(b_last + m_prev - m_new)
        w_col = jnp.exp(b_last - b_col + i_col - m_new) * scale
        vw = (v.astype(F32) * w_col).astype(BF16)
        upd = lax.dot_general(k, vw, (((0,), (0,)), ((), ())), preferred_element_type=F32)
        c_new = upd + c_ref[hh] * decay
        c_ref[hh] = c_new
        cb_ref[hh] = c_new.astype(BF16)
        n_ref[hh] = n_ref[hh] * decay + jnp.sum(k.astype(F32) * w_col, axis=0, keepdims=True)
        m_ref[hh] = jnp.broadcast_to(m_new, m_ref.shape[1:])

        mu = jnp.mean(hc, axis=-1, keepdims=True)
        ctr = hc - mu
        var = jnp.mean(ctr * ctr, axis=-1, keepdims=True)
        hn = ctr * lax.rsqrt(var + NORM_EPS)
        cell = (1.0 / (1.0 + jnp.exp(-op_ref[:, cols]))) * (hn * gn_ref[:, cols])
        o_ref[:, cols] = ((cell + sk_ref[:, cols] * xc_ref[:, cols])
                          * _silu(z_ref[:, cols])).astype(o_ref.dtype)


def _mlstm(q, k, v, gates, up, xc, gn, skip, w_stack, layer):
    S, E = q.shape
    H, L, G = MLSTM_HEADS, MLSTM_CHUNK, MLSTM_GROUP
    DH = E // H
    NCH = S // L
    HG = H // G
    w_in_spec, w_out_spec, cast_blocks = _side_cast_specs(w_stack, layer, HG * NCH,
                                                          lambda h, c: h * NCH + c)
    gt = gates.T.reshape(2 * H, NCH, 1, L)
    blk = lambda col0: pl.BlockSpec((L, G * DH), lambda h, c: (c, col0 + h))
    vec = pl.BlockSpec((1, G * DH), lambda h, c: (0, h))
    return pl.pallas_call(
        functools.partial(_mlstm_body, cast_blocks=cast_blocks),
        grid=(HG, NCH),
        in_specs=[
            blk(0), blk(0), blk(0),
            pl.BlockSpec((G, 1, 1, L), lambda h, c: (h, c, 0, 0)),
            pl.BlockSpec((G, 1, 1, L), lambda h, c: (HG + h, c, 0, 0)),
            blk(2 * HG), blk(0), blk(HG),
            vec, vec,
            w_in_spec,
        ],
        out_specs=[blk(0), w_out_spec],
        out_shape=[jax.ShapeDtypeStruct((S, E), BF16), jax.ShapeDtypeStruct(w_stack.shape[1:], BF16)],
        scratch_shapes=[
            pltpu.VMEM((G, DH, DH), F32),
            pltpu.VMEM((G, DH, DH), BF16),
            pltpu.VMEM((G, 1, DH), F32),
            pltpu.VMEM((G, 8, LANES), F32),
        ],
        compiler_params=_params("arbitrary", "arbitrary"),
        name="mlstm_chunkwise",
    )(q, k, v, gt, gt, up, xc, up, gn.reshape(1, E).astype(F32), skip.reshape(1, E).astype(F32),
      w_stack)


def _even_layer(h, xg, r, j, rel_bias, w_in, q_gain, k_gain, pool_w, pool_scale, w_out, next_gain):
    mix = w_out.shape[1]
    pool_width = pool_w.shape[0] * pool_w.shape[1]
    attn_width = mix - pool_width
    n_heads = attn_width // ATTN_HEAD_DIM
    u_col = 3 * attn_width
    z_col = u_col + pool_width
    proj = _matmul([xg], w_in, j, row_scale=r, tm=IN_PROJ_ROWS)
    attn, w_out_bf16 = _attention(proj, rel_bias, q_gain, k_gain, n_heads,
                                  z_col // ATTN_HEAD_DIM, w_out, j)
    pool = _pool(proj, pool_w, pool_scale, u_col, z_col + attn_width)
    return _matmul([attn, pool], w_out_bf16, j, residual=h, next_gain=next_gain, tn=OUT_PROJ_COLS)


def _odd_layer(h, xg, r, j, w_up, conv_w, conv_b, wq, wk, wv, w_if, b_if, gn, skip, w_down,
               next_gain):
    up = _matmul([xg], w_up, j, row_scale=r, tm=IN_PROJ_ROWS)
    xc, q, k, v, gates = _mlstm_front(up, conv_w, conv_b, wq, wk, wv, w_if, b_if)
    out, w_down_bf16 = _mlstm(q, k, v, gates, up, xc, gn, skip, w_down, j)
    return _matmul([out], w_down_bf16, j, residual=h, next_gain=next_gain, tn=OUT_PROJ_COLS)


def kernel(x, rel_bias, e_norm, e_w_in, e_q_gain, e_k_gain, e_pool_w, e_pool_scale, e_w_out,
           o_norm, o_w_up, o_conv_w, o_conv_b, o_wq, o_wk, o_wv, o_w_if, o_b_if, o_gn, o_skip,
           o_w_down):
    B, S, D = x.shape
    depth = e_norm.shape[0] + o_norm.shape[0]
    outs = []
    for b in range(B):
        h = x[b]
        gains = [(e_norm if layer % 2 == 0 else o_norm)[layer // 2] for layer in range(depth)]
        xg, r = _norm_factors(h, gains[0])
        for layer in range(depth):
            j = layer // 2
            next_gain = gains[layer + 1] if layer + 1 < depth else None
            if layer % 2 == 0:
                res = _even_layer(h, xg, r, j, rel_bias, e_w_in, e_q_gain[j], e_k_gain[j],
                                  e_pool_w[j], e_pool_scale[j], e_w_out, next_gain)
            else:
                res = _odd_layer(h, xg, r, j, o_w_up, o_conv_w[j], o_conv_b[j], o_wq[j], o_wk[j],
                                 o_wv[j], o_w_if[j], o_b_if[j], o_gn[j], o_skip[j], o_w_down,
                                 next_gain)
            h, xg, r = res if next_gain is not None else (res, None, None)
        outs.append(h)
    return jnp.stack(outs)
```

```python
import functools

import numpy as np
import jax
import jax.numpy as jnp
from jax import lax
from jax.experimental import pallas as pl
from jax.experimental.pallas import tpu as pltpu

F32 = jnp.float32
BF16 = jnp.bfloat16

NORM_EPS = 1e-6
IN_PROJ_ROWS = 2048
OUT_PROJ_COLS = 256
LOG2E = 1.4426950408889634
LANES = 128
ATTN_HEAD_DIM = 128
ATTN_BLOCK = 128
DILATIONS = (1, 4, 16)
ATTN_CHUNK = ATTN_BLOCK * DILATIONS[-1]
POOL_WINDOWS = (2, 4, 8, 16)
POOL_HALO = 16
REL_BUCKETS = 32
REL_MAX_DIST = 2048
MLSTM_HEADS = 8
MLSTM_CHUNK = 256
MLSTM_GROUP = 2
CONV_WIDTH = 4
CONV_HALO = 8
QKV_TILE = 128
VMEM_LIMIT = 56 * 1024 * 1024


def _params(*sem):
    return pltpu.CompilerParams(dimension_semantics=sem, vmem_limit_bytes=VMEM_LIMIT)


def _silu(x):
    return x * (1.0 / (1.0 + jnp.exp(-x)))


def _mm_body(*refs, n_a, has_res, has_scale, has_next, n_cols):
    a_refs = refs[:n_a]
    w_refs = refs[n_a:2 * n_a]
    p = 2 * n_a
    res_ref = refs[p] if has_res else None
    p += int(has_res)
    scale_ref = refs[p] if has_scale else None
    p += int(has_scale)
    gain_ref = refs[p] if has_next else None
    p += int(has_next)
    o_ref = refs[p]
    acc = None
    for a_ref, w_ref in zip(a_refs, w_refs):
        w = w_ref[...]
        d = jnp.dot(a_ref[...], w if w.dtype == BF16 else w.astype(BF16),
                    preferred_element_type=F32)
        acc = d if acc is None else acc + d
    if has_scale:
        acc = acc * scale_ref[:, 0:1]
    if has_res:
        acc = res_ref[...] + acc
    o_ref[...] = acc.astype(o_ref.dtype)
    if has_next:
        xg_ref, r_ref, ssq_ref = refs[p + 1], refs[p + 2], refs[p + 3]
        j = pl.program_id(1)
        xg_ref[...] = (acc * gain_ref[...]).astype(BF16)
        part = jnp.broadcast_to(jnp.sum(acc * acc, axis=-1, keepdims=True), ssq_ref.shape)

        @pl.when(j == 0)
        def _():
            ssq_ref[...] = part

        @pl.when(j > 0)
        def _():
            ssq_ref[...] += part

        @pl.when(j == pl.num_programs(1) - 1)
        def _():
            r_ref[...] = lax.rsqrt(ssq_ref[...] * (1.0 / n_cols) + NORM_EPS)


def _matmul(a_list, w_stack, layer, residual=None, row_scale=None, next_gain=None,
            out_dtype=F32, tm=1024, tn=512, a_buffers=1):
    M = a_list[0].shape[0]
    K, N = w_stack.shape[-2:]
    tm, tn = min(tm, M), min(tn, N)
    assert M % tm == 0 and N % tn == 0 and sum(a.shape[1] for a in a_list) == K
    in_specs, w_specs, row0 = [], [], 0
    for a in a_list:
        kp = a.shape[1]
        assert row0 % kp == 0
        in_specs.append(pl.BlockSpec((tm, kp), lambda i, j: (i, 0),
                                     pipeline_mode=pl.Buffered(a_buffers)))
        if w_stack.ndim == 3:
            w_specs.append(pl.BlockSpec((None, kp, tn), lambda i, j, rb=row0 // kp: (layer, rb, j)))
        else:
            w_specs.append(pl.BlockSpec((kp, tn), lambda i, j, rb=row0 // kp: (rb, j)))
        row0 += kp
    in_specs += w_specs
    args = list(a_list) + [w_stack] * len(a_list)
    tile = pl.BlockSpec((tm, tn), lambda i, j: (i, j))
    rows = pl.BlockSpec((tm, LANES), lambda i, j: (i, 0))
    if residual is not None:
        in_specs.append(tile)
        args.append(residual)
    if row_scale is not None:
        in_specs.append(rows)
        args.append(row_scale)
    out_specs, out_shape, scratch = tile, jax.ShapeDtypeStruct((M, N), out_dtype), []
    if next_gain is not None:
        in_specs.append(pl.BlockSpec((1, tn), lambda i, j: (0, j)))
        args.append(next_gain.reshape(1, N).astype(F32))
        out_specs = [tile, tile, rows]
        out_shape = [out_shape, jax.ShapeDtypeStruct((M, N), BF16), jax.ShapeDtypeStruct((M, LANES), F32)]
        scratch = [pltpu.VMEM((tm, LANES), F32)]
    return pl.pallas_call(
        functools.partial(_mm_body, n_a=len(a_list), has_res=residual is not None,
                          has_scale=row_scale is not None, has_next=next_gain is not None, n_cols=N),
        grid=(M // tm, N // tn),
        in_specs=in_specs,
        out_specs=out_specs,
        out_shape=out_shape,
        scratch_shapes=scratch,
        compiler_params=_params("arbitrary", "arbitrary"),
        name="matmul",
    )(*args)


def _side_cast_specs(w_stack, layer, n_steps, step_of):
    K, N = w_stack.shape[1:]
    n_blocks = 1 << (n_steps.bit_length() - 1)
    while K % n_blocks or (K // n_blocks) % 16:
        n_blocks //= 2
    rb = K // n_blocks
    idx = lambda *g: jnp.minimum(step_of(*g), n_blocks - 1)
    return (pl.BlockSpec((None, rb, N), lambda *g: (layer, idx(*g), 0)),
            pl.BlockSpec((rb, N), lambda *g: (idx(*g), 0)), n_blocks)


def _side_cast(wf_ref, wb_ref, step, n_blocks):
    @pl.when(step < n_blocks)
    def _():
        wb_ref[...] = wf_ref[...].astype(BF16)


def _norm_factors_body(x_ref, g_ref, xg_ref, r_ref):
    x = x_ref[...]
    xg_ref[...] = (x * g_ref[...]).astype(BF16)
    ms = jnp.mean(x * x, axis=-1, keepdims=True)
    r_ref[...] = jnp.broadcast_to(lax.rsqrt(ms + NORM_EPS), r_ref.shape)


def _norm_factors(x, gain, tm=256):
    M, D = x.shape
    tm = min(tm, M)
    return pl.pallas_call(
        _norm_factors_body,
        grid=(M // tm,),
        in_specs=[pl.BlockSpec((tm, D), lambda i: (i, 0)),
                  pl.BlockSpec((1, D), lambda i: (0, 0))],
        out_specs=[pl.BlockSpec((tm, D), lambda i: (i, 0)),
                   pl.BlockSpec((tm, LANES), lambda i: (i, 0))],
        out_shape=[jax.ShapeDtypeStruct((M, D), BF16), jax.ShapeDtypeStruct((M, LANES), F32)],
        compiler_params=_params("parallel"),
        name="norm_factors",
    )(x, gain.reshape(1, D).astype(F32))


def _t5_bucket(dist):
    max_exact = REL_BUCKETS // 2
    safe = np.maximum(dist, 1).astype(np.float32)
    large = max_exact + (np.log(safe / max_exact) / np.log(REL_MAX_DIST / max_exact)
                         * (REL_BUCKETS - max_exact)).astype(np.int32)
    large = np.minimum(large, REL_BUCKETS - 1)
    return np.where(dist < max_exact, dist, large).astype(np.int32)


def _attn_bucket_table():
    B = ATTN_BLOCK
    rel = B - np.arange(2 * B)
    band = (rel >= 0) & (rel <= B)
    rows = [np.where(band, _t5_bucket(np.clip(rel, 0, None) * d), -1) for d in DILATIONS]
    return np.broadcast_to(np.stack(rows)[:, None, :], (len(DILATIONS), 8, 2 * B)).astype(np.int32)


def _attn_body(relb_ref, bkt_ref, q_ref, k_ref, v_ref, z_ref, qg_ref, kg_ref, wf_ref,
               o_ref, wb_ref, bias_ref, qn_ref, q4_ref, kn_ref, k4_ref, vn_ref, v4_ref,
               o1_ref, m1_ref, l1_ref, o2_ref, m2_ref, l2_ref, o3_ref, m3_ref, l3_ref, out_ref,
               *, cast_blocks):
    B = ATTN_BLOCK
    C = ATTN_CHUNK
    R4 = DILATIONS[1]
    Q = C // R4
    h = pl.program_id(0)
    c = pl.program_id(1)
    n_dil = len(DILATIONS)
    cur = c % 2
    prv = 1 - cur

    @pl.when(c == 0)
    def _():
        col = lax.broadcasted_iota(jnp.int32, (B, 2 * B), 1)
        for t in range(n_dil):
            bkt = bkt_ref[t]
            row = jnp.full(bkt.shape, -jnp.inf, F32)
            for b in range(REL_BUCKETS):
                row = jnp.where(bkt == b, relb_ref[b, h] * LOG2E, row)
            bias = pltpu.roll(jnp.broadcast_to(row[0:1, :], (B, 2 * B)), 0, 1,
                              stride=1, stride_axis=0)
            bias_ref[t] = bias
            bias_ref[t + n_dil] = jnp.where(col >= B, bias, -jnp.inf)

    @pl.when(jnp.logical_and(h == 0, c == 0))
    def _():
        kn_ref[1] = jnp.zeros(kn_ref.shape[1:], F32)
        vn_ref[1] = jnp.zeros(vn_ref.shape[1:], F32)
        k4_ref[1] = jnp.zeros(k4_ref.shape[1:], F32)
        v4_ref[1] = jnp.zeros(v4_ref.shape[1:], F32)

    def _norm(x, g):
        ms = jnp.mean(x * x, axis=-1, keepdims=True)
        return x * lax.rsqrt(ms + NORM_EPS) * g

    qn_ref[...] = _norm(q_ref[...], qg_ref[...]) * (ATTN_HEAD_DIM ** -0.5 * LOG2E)
    kn_ref[cur] = _norm(k_ref[...], kg_ref[...])
    vn_ref[cur] = v_ref[...]
    for r4 in range(R4):
        rows = pl.ds(r4, Q, stride=R4)
        q4_ref[r4] = qn_ref[rows, :]
        k4_ref[cur, r4] = kn_ref[cur, rows, :]
        v4_ref[cur, r4] = vn_ref[cur, rows, :]

    first_chunk = jnp.where(c == 0, 1, 0)

    def block(q, k, v, bias):
        s = lax.dot_general(q.astype(BF16), k.astype(BF16), (((1,), (1,)), ((), ())),
                            preferred_element_type=F32) + bias
        m = jnp.max(s, axis=-1, keepdims=True)
        p = jnp.exp2(s - m).astype(BF16)
        v_aug = jnp.concatenate([v.astype(BF16), jnp.ones((2 * B, LANES), BF16)], axis=1)
        o_aug = jnp.dot(p, v_aug, preferred_element_type=F32)
        return o_aug[:, :ATTN_HEAD_DIM], m, o_aug[:, ATTN_HEAD_DIM:]

    def bcast(x):
        return jnp.broadcast_to(x, (B, LANES))

    for b in range(C // B):
        rb = slice(b * B, (b + 1) * B)
        if b == 0:
            ka, va = kn_ref[prv, C - B:C, :], vn_ref[prv, C - B:C, :]
            bias = bias_ref[n_dil * first_chunk]
        else:
            ka, va = kn_ref[cur, (b - 1) * B:b * B, :], vn_ref[cur, (b - 1) * B:b * B, :]
            bias = bias_ref[0]
        k = jnp.concatenate([ka, kn_ref[cur, rb, :]], axis=0)
        v = jnp.concatenate([va, vn_ref[cur, rb, :]], axis=0)
        o, m, l = block(qn_ref[rb, :], k, v, bias)
        o1_ref[rb, :] = o
        m1_ref[rb, :] = bcast(m)
        l1_ref[rb, :] = l

    for sub in range(Q // B):
        rb = slice(sub * B, (sub + 1) * B)
        bias = bias_ref[1 + n_dil * first_chunk] if sub == 0 else bias_ref[1]
        for r4 in range(R4):
            if sub == 0:
                ka, va = k4_ref[prv, r4, Q - B:Q, :], v4_ref[prv, r4, Q - B:Q, :]
            else:
                ra = slice((sub - 1) * B, sub * B)
                ka, va = k4_ref[cur, r4, ra, :], v4_ref[cur, r4, ra, :]
            k = jnp.concatenate([ka, k4_ref[cur, r4, rb, :]], axis=0)
            v = jnp.concatenate([va, v4_ref[cur, r4, rb, :]], axis=0)
            o, m, l = block(q4_ref[r4, rb, :], k, v, bias)
            o2_ref[r4, rb, :] = o
            m2_ref[r4, rb, :] = bcast(m)
            l2_ref[r4, rb, :] = l

    bias = bias_ref[2 + n_dil * first_chunk]
    for o4 in range(Q // B):
        rows = pl.ds(o4, B, stride=R4)
        for r4 in range(R4):
            k = jnp.concatenate([k4_ref[prv, r4, rows, :], k4_ref[cur, r4, rows, :]], axis=0)
            v = jnp.concatenate([v4_ref[prv, r4, rows, :], v4_ref[cur, r4, rows, :]], axis=0)
            o, m, l = block(q4_ref[r4, rows, :], k, v, bias)
            o3_ref[r4, rows, :] = o
            m3_ref[r4, rows, :] = bcast(m)
            l3_ref[r4, rows, :] = l

    def body_merge(sub, carry):
        r0 = pl.multiple_of(sub * B, B)
        for r4 in range(R4):
            nat = pl.ds(sub * (B * R4) + r4, B, stride=R4)
            m1, m2, m3 = m1_ref[nat, :], m2_ref[r4, pl.ds(r0, B), :], m3_ref[r4, pl.ds(r0, B), :]
            mx = jnp.maximum(jnp.maximum(m1, m2), m3)
            w1, w2, w3 = jnp.exp2(m1 - mx), jnp.exp2(m2 - mx), jnp.exp2(m3 - mx)
            num = (o1_ref[nat, :] * w1 + o2_ref[r4, pl.ds(r0, B), :] * w2
                   + o3_ref[r4, pl.ds(r0, B), :] * w3)
            den = (l1_ref[nat, :] * w1 + l2_ref[r4, pl.ds(r0, B), :] * w2
                   + l3_ref[r4, pl.ds(r0, B), :] * w3)
            out_ref[nat, :] = num / den
        return carry

    lax.fori_loop(0, Q // B, body_merge, 0)

    o_ref[...] = (out_ref[...] * _silu(z_ref[...])).astype(o_ref.dtype)
    _side_cast(wf_ref, wb_ref, h * pl.num_programs(1) + c, cast_blocks)


def _attention(proj, rel_bias, q_gain, k_gain, n_heads, z_col, w_stack, layer):
    S = proj.shape[0]
    C, B, Dh = ATTN_CHUNK, ATTN_BLOCK, ATTN_HEAD_DIM
    assert S % C == 0
    H = n_heads
    NC = S // C
    w_in_spec, w_out_spec, cast_blocks = _side_cast_specs(w_stack, layer, H * NC,
                                                          lambda h, c: h * NC + c)
    bkt = jnp.asarray(_attn_bucket_table())
    R4 = DILATIONS[1]
    blk = lambda f: pl.BlockSpec((C, Dh), f)
    nat = pltpu.VMEM((C, Dh), F32)
    mod4 = pltpu.VMEM((R4, C // R4, Dh), F32)
    return pl.pallas_call(
        functools.partial(_attn_body, cast_blocks=cast_blocks),
        grid=(H, NC),
        in_specs=[
            pl.BlockSpec(memory_space=pltpu.SMEM),
            pl.BlockSpec(bkt.shape, lambda h, c: (0, 0, 0)),
            blk(lambda h, c: (c, h)),
            blk(lambda h, c: (c, H + h)),
            blk(lambda h, c: (c, 2 * H + h)),
            blk(lambda h, c: (c, z_col + h)),
            pl.BlockSpec((1, Dh), lambda h, c: (0, 0)),
            pl.BlockSpec((1, Dh), lambda h, c: (0, 0)),
            w_in_spec,
        ],
        out_specs=[blk(lambda h, c: (c, h)), w_out_spec],
        out_shape=[jax.ShapeDtypeStruct((S, H * Dh), BF16),
                   jax.ShapeDtypeStruct(w_stack.shape[1:], BF16)],
        scratch_shapes=[
            pltpu.VMEM((2 * len(DILATIONS), B, 2 * B), F32),
            nat, mod4,
            pltpu.VMEM((2, C, Dh), F32), pltpu.VMEM((2, R4, C // R4, Dh), F32),
            pltpu.VMEM((2, C, Dh), F32), pltpu.VMEM((2, R4, C // R4, Dh), F32),
            nat, nat, nat,
            mod4, mod4, mod4,
            mod4, mod4, mod4,
            nat,
        ],
        compiler_params=_params("arbitrary", "arbitrary"),
        name="dilated_attention",
    )(rel_bias.astype(F32), bkt, proj, proj, proj, proj,
      q_gain.reshape(1, Dh).astype(F32), k_gain.reshape(1, Dh).astype(F32), w_stack)


def _pool_body(u_ref, halo_ref, z_ref, w_ref, sc_ref, o_ref, *, group_dim):
    T = u_ref.shape[0]
    i = pl.program_id(0)
    G = len(POOL_WINDOWS)
    halo_on = jnp.where(i > 0, 1.0, 0.0)
    pos = (i * T + lax.broadcasted_iota(jnp.int32, (T, 1), 0) + 1).astype(F32)
    for g, w in enumerate(POOL_WINDOWS):
        cols = slice(g * group_dim, (g + 1) * group_dim)
        x = u_ref[:, cols]
        e = jnp.concatenate([halo_ref[:, cols] * halo_on, x], axis=0)
        width = 1
        while width < w:
            e = e[width:, :] + e[:-width, :]
            width *= 2
        off = POOL_HALO - (w - 1)
        win = e[off:off + T, :]
        y = win / jnp.minimum(pos, float(w)) - x
        yp = jnp.dot(y.astype(BF16), w_ref[g].astype(BF16), preferred_element_type=F32)
        o_ref[:, cols] = (yp * sc_ref[:, cols] * _silu(z_ref[:, cols])).astype(o_ref.dtype)


def _pool(proj, pool_w, pool_scale, u_col, z_col, tile=256):
    S = proj.shape[0]
    G, Cg, _ = pool_w.shape
    P = G * Cg
    T = min(tile, S)
    assert u_col % P == 0 and z_col % P == 0 and T % POOL_HALO == 0
    return pl.pallas_call(
        functools.partial(_pool_body, group_dim=Cg),
        grid=(S // T,),
        in_specs=[
            pl.BlockSpec((T, P), lambda i: (i, u_col // P)),
            pl.BlockSpec((POOL_HALO, P), lambda i: (jnp.maximum(i * (T // POOL_HALO) - 1, 0), u_col // P)),
            pl.BlockSpec((T, P), lambda i: (i, z_col // P)),
            pl.BlockSpec((G, Cg, Cg), lambda i: (0, 0, 0)),
            pl.BlockSpec((1, P), lambda i: (0, 0)),
        ],
        out_specs=pl.BlockSpec((T, P), lambda i: (i, 0)),
        out_shape=jax.ShapeDtypeStruct((S, P), BF16),
        compiler_params=_params("parallel"),
        name="multiscale_pool",
    )(proj, proj, proj, pool_w.astype(F32), pool_scale.reshape(1, P).astype(F32))


def _front_body(xm_ref, halo_ref, cw_ref, cb_ref, wq_ref, wk_ref, wv_ref, wif_ref, bif_ref,
                xc_ref, q_ref, k_ref, v_ref, g_ref):
    T, TC = xm_ref.shape
    i = pl.program_id(0)
    j = pl.program_id(1)
    xm = xm_ref[...]
    halo = halo_ref[...] * jnp.where(i > 0, 1.0, 0.0)
    e = jnp.concatenate([halo, xm], axis=0)
    conv = cb_ref[...]
    for t in reversed(range(CONV_WIDTH)):
        off = CONV_HALO - (CONV_WIDTH - 1) + t
        conv = conv + e[off:off + T, :] * cw_ref[t:t + 1, :]
    xc = _silu(conv)
    xc_ref[...] = xc
    xcb = xc.astype(BF16)
    xmb = xm.astype(BF16)
    gates = jnp.zeros(g_ref.shape, F32)
    W = wq_ref.shape[-1]
    for n in range(TC // W):
        cols = slice(n * W, (n + 1) * W)
        wqk = jnp.concatenate([wq_ref[n], wk_ref[n]], axis=1)
        yqk = jnp.dot(xcb[:, cols], wqk, preferred_element_type=F32).astype(BF16)
        q_ref[:, cols] = yqk[:, :W]
        k_ref[:, cols] = yqk[:, W:]
        wif_qk = jnp.concatenate([wif_ref[0, cols, :], wif_ref[1, cols, :]], axis=0)
        yv = jnp.dot(xmb[:, cols], wv_ref[n], preferred_element_type=F32).astype(BF16)
        v_ref[:, cols] = yv
        gates = (gates + jnp.dot(yqk, wif_qk, preferred_element_type=F32)
                 + jnp.dot(yv, wif_ref[2, cols, :], preferred_element_type=F32))

    @pl.when(j == 0)
    def _():
        g_ref[...] = bif_ref[...] + gates

    @pl.when(j > 0)
    def _():
        g_ref[...] += gates


def _block_diag_dense(w, width):
    nb, bs, _ = w.shape
    per = width // bs
    wg = w.reshape(nb // per, per, bs, bs)
    eye = jnp.eye(per, dtype=w.dtype)
    dense = jnp.einsum('gpcd,pq->gpcqd', wg, eye)
    return dense.reshape(nb // per, width, width).astype(BF16)


def _mlstm_front(up, conv_w, conv_b, wq, wk, wv, w_if, b_if, tile=512, tcol=1024):
    S = up.shape[0]
    E = conv_w.shape[1]
    T, TC = min(tile, S), min(tcol, E)
    NG = w_if.shape[1]
    W = min(QKV_TILE, TC)
    nt = TC // W
    wdense = [_block_diag_dense(w, W) for w in (wq, wk, wv)]
    wif = w_if.reshape(3, E, NG).astype(BF16)
    row = lambda: pl.BlockSpec((T, TC), lambda i, j: (i, j))
    return pl.pallas_call(
        _front_body,
        grid=(S // T, E // TC),
        in_specs=[
            row(),
            pl.BlockSpec((CONV_HALO, TC), lambda i, j: (jnp.maximum(i * (T // CONV_HALO) - 1, 0), j)),
            pl.BlockSpec((CONV_WIDTH, TC), lambda i, j: (0, j)),
            pl.BlockSpec((1, TC), lambda i, j: (0, j)),
            pl.BlockSpec((nt, W, W), lambda i, j: (j, 0, 0)),
            pl.BlockSpec((nt, W, W), lambda i, j: (j, 0, 0)),
            pl.BlockSpec((nt, W, W), lambda i, j: (j, 0, 0)),
            pl.BlockSpec((3, TC, NG), lambda i, j: (0, j, 0)),
            pl.BlockSpec((1, NG), lambda i, j: (0, 0)),
        ],
        out_specs=[row(), row(), row(), row(), pl.BlockSpec((T, NG), lambda i, j: (i, 0))],
        out_shape=[jax.ShapeDtypeStruct((S, E), F32)] + [jax.ShapeDtypeStruct((S, E), BF16)] * 3
        + [jax.ShapeDtypeStruct((S, NG), F32)],
        compiler_params=_params("parallel", "arbitrary"),
        name="mlstm_front",
    )(up, up, conv_w.astype(F32), conv_b.reshape(1, E).astype(F32), *wdense, wif,
      b_if.reshape(1, NG).astype(F32))


def _mlstm_body(q_ref, k_ref, v_ref, ig_ref, fg_ref, op_ref, xc_ref, z_ref, gn_ref, sk_ref,
                wf_ref, o_ref, wb_ref, c_ref, cb_ref, n_ref, m_ref, *, cast_blocks):
    L = q_ref.shape[0]
    G, DK, _ = c_ref.shape
    c = pl.program_id(1)
    _side_cast(wf_ref, wb_ref, pl.program_id(0) * pl.num_programs(1) + c, cast_blocks)

    @pl.when(c == 0)
    def _():
        c_ref[...] = jnp.zeros_like(c_ref)
        cb_ref[...] = jnp.zeros_like(cb_ref)
        n_ref[...] = jnp.zeros_like(n_ref)
        m_ref[...] = jnp.full(m_ref.shape, -1e30, F32)

    ri = lax.broadcasted_iota(jnp.int32, (L, L), 0)
    cj = lax.broadcasted_iota(jnp.int32, (L, L), 1)
    lane8 = lax.broadcasted_iota(jnp.int32, (8, L), 1)
    scale = DK ** -0.5

    for hh in range(G):
        cols = slice(hh * DK, (hh + 1) * DK)
        i_row = ig_ref[hh, 0]
        f_row = fg_ref[hh, 0]
        lf_row = jnp.minimum(f_row, 0.0) - jnp.log1p(jnp.exp(-jnp.abs(f_row)))
        b8 = jnp.broadcast_to(lf_row, (8, L))
        sh = 1
        while sh < L:
            b8 = b8 + jnp.where(lane8 >= sh, pltpu.roll(b8, sh, 1), 0.0)
            sh *= 2
        b_row = b8[0:1, :]
        stacked = jnp.where(ri == 0, jnp.broadcast_to(b_row, (L, L)),
                            jnp.where(ri == 1, jnp.broadcast_to(i_row, (L, L)), 0.0))
        stacked_t = stacked.T
        b_col = stacked_t[:, 0:1]
        i_col = stacked_t[:, 1:2]

        m_prev = m_ref[hh, 0:1, 0:1]
        log_d = jnp.where(cj <= ri, b_col - b_row + i_row, -jnp.inf)
        log_inter = b_col + m_prev
        m_t = jnp.maximum(jnp.max(log_d, axis=-1, keepdims=True), log_inter)
        dmat = jnp.exp(log_d - m_t) * scale
        g = jnp.exp(log_inter - m_t)

        q = q_ref[:, cols]
        k = k_ref[:, cols]
        v = v_ref[:, cols]
        s = lax.dot_general(q, k, (((1,), (1,)), ((), ())), preferred_element_type=F32) * dmat
        inter = jnp.dot(q, cb_ref[hh], preferred_element_type=F32)
        num = jnp.dot(s.astype(BF16), v, preferred_element_type=F32) + g * inter
        qn = jnp.sum(q.astype(F32) * n_ref[hh], axis=-1, keepdims=True)
        den = jnp.sum(s, axis=-1, keepdims=True) + g * qn
        hc = num / jnp.maximum(jnp.abs(den), jnp.exp(-m_t))

        m_new = m_t[L - 1:L, :]
        b_last = b_col[L - 1:L, :]
        decay = jnp.exp(b_last + m_prev - m_new)
        w_col = jnp.exp(b_last - b_col + i_col - m_new) * scale
        vw = (v.astype(F32) * w_col).astype(BF16)
        upd = lax.dot_general(k, vw, (((0,), (0,)), ((), ())), preferred_element_type=F32)
        c_new = upd + c_ref[hh] * decay
        c_ref[hh] = c_new
        cb_ref[hh] = c_new.astype(BF16)
        n_ref[hh] = n_ref[hh] * decay + jnp.sum(k.astype(F32) * w_col, axis=0, keepdims=True)
        m_ref[hh] = jnp.broadcast_to(m_new, m_ref.shape[1:])

        mu = jnp.mean(hc, axis=-1, keepdims=True)
        ctr = hc - mu
        var = jnp.mean(ctr * ctr, axis=-1, keepdims=True)
        hn = ctr * lax.rsqrt(var + NORM_EPS)
        cell = (1.0 / (1.0 + jnp.exp(-op_ref[:, cols]))) * (hn * gn_ref[:, cols])
        o_ref[:, cols] = ((cell + sk_ref[:, cols] * xc_ref[:, cols])
                          * _silu(z_ref[:, cols])).astype(o_ref.dtype)


def _mlstm(q, k, v, gates, up, xc, gn, skip, w_stack, layer):
    S, E = q.shape
    H, L, G = MLSTM_HEADS, MLSTM_CHUNK, MLSTM_GROUP
    DH = E // H
    NCH = S // L
    HG = H // G
    w_in_spec, w_out_spec, cast_blocks = _side_cast_specs(w_stack, layer, HG * NCH,
                                                          lambda h, c: h * NCH + c)
    gt = gates.T.reshape(2 * H, NCH, 1, L)
    blk = lambda col0: pl.BlockSpec((L, G * DH), lambda h, c: (c, col0 + h))
    vec = pl.BlockSpec((1, G * DH), lambda h, c: (0, h))
    return pl.pallas_call(
        functools.partial(_mlstm_body, cast_blocks=cast_blocks),
        grid=(HG, NCH),
        in_specs=[
            blk(0), blk(0), blk(0),
            pl.BlockSpec((G, 1, 1, L), lambda h, c: (h, c, 0, 0)),
            pl.BlockSpec((G, 1, 1, L), lambda h, c: (HG + h, c, 0, 0)),
            blk(2 * HG), blk(0), blk(HG),
            vec, vec,
            w_in_spec,
        ],
        out_specs=[blk(0), w_out_spec],
        out_shape=[jax.ShapeDtypeStruct((S, E), BF16), jax.ShapeDtypeStruct(w_stack.shape[1:], BF16)],
        scratch_shapes=[
            pltpu.VMEM((G, DH, DH), F32),
            pltpu.VMEM((G, DH, DH), BF16),
            pltpu.VMEM((G, 1, DH), F32),
            pltpu.VMEM((G, 8, LANES), F32),
        ],
        compiler_params=_params("arbitrary", "arbitrary"),
        name="mlstm_chunkwise",
    )(q, k, v, gt, gt, up, xc, up, gn.reshape(1, E).astype(F32), skip.reshape(1, E).astype(F32),
      w_stack)


def _even_layer(h, xg, r, j, rel_bias, w_in, q_gain, k_gain, pool_w, pool_scale, w_out, next_gain):
    mix = w_out.shape[1]
    pool_width = pool_w.shape[0] * pool_w.shape[1]
    attn_width = mix - pool_width
    n_heads = attn_width // ATTN_HEAD_DIM
    u_col = 3 * attn_width
    z_col = u_col + pool_width
    proj = _matmul([xg], w_in, j, row_scale=r, tm=IN_PROJ_ROWS)
    attn, w_out_bf16 = _attention(proj, rel_bias, q_gain, k_gain, n_heads,
                                  z_col // ATTN_HEAD_DIM, w_out, j)
    pool = _pool(proj, pool_w, pool_scale, u_col, z_col + attn_width)
    return _matmul([attn, pool], w_out_bf16, j, residual=h, next_gain=next_gain,
                   tn=OUT_PROJ_COLS, a_buffers=2)


def _odd_layer(h, xg, r, j, w_up, conv_w, conv_b, wq, wk, wv, w_if, b_if, gn, skip, w_down,
               next_gain):
    up = _matmul([xg], w_up, j, row_scale=r, tm=IN_PROJ_ROWS)
    xc, q, k, v, gates = _mlstm_front(up, conv_w, conv_b, wq, wk, wv, w_if, b_if)
    out, w_down_bf16 = _mlstm(q, k, v, gates, up, xc, gn, skip, w_down, j)
    return _matmul([out], w_down_bf16, j, residual=h, next_gain=next_gain,
                   tn=OUT_PROJ_COLS, a_buffers=2)


def kernel(x, rel_bias, e_norm, e_w_in, e_q_gain, e_k_gain, e_pool_w, e_pool_scale, e_w_out,
           o_norm, o_w_up, o_conv_w, o_conv_b, o_wq, o_wk, o_wv, o_w_if, o_b_if, o_gn, o_skip,
           o_w_down):
    B, S, D = x.shape
    depth = e_norm.shape[0] + o_norm.shape[0]
    outs = []
    for b in range(B):
        h = x[b]
        gains = [(e_norm if layer % 2 == 0 else o_norm)[layer // 2] for layer in range(depth)]
        xg, r = _norm_factors(h, gains[0])
        for layer in range(depth):
            j = layer // 2
            next_gain = gains[layer + 1] if layer + 1 < depth else None
            if layer % 2 == 0:
                res = _even_layer(h, xg, r, j, rel_bias, e_w_in, e_q_gain[j], e_k_gain[j],
                                  e_pool_w[j], e_pool_scale[j], e_w_out, next_gain)
            else:
                res = _odd_layer(h, xg, r, j, o_w_up, o_conv_w[j], o_conv_b[j], o_wq[j], o_wk[j],
                                 o_wv[j], o_w_if[j], o_b_if[j], o_gn[j], o_skip[j], o_w_down,
                                 next_gain)
            h, xg, r = res if next_gain is not None else (res, None, None)
        outs.append(h)
    return jnp.stack(outs)
```

```python
import functools

import numpy as np
import jax
import jax.numpy as jnp
from jax import lax
from jax.experimental import pallas as pl
from jax.experimental.pallas import tpu as pltpu

F32 = jnp.float32
BF16 = jnp.bfloat16

NORM_EPS = 1e-6
IN_PROJ_ROWS = 2048
OUT_PROJ_COLS = 256
LOG2E = 1.4426950408889634
LANES = 128
ATTN_HEAD_DIM = 128
ATTN_BLOCK = 128
DILATIONS = (1, 4, 16)
ATTN_CHUNK = ATTN_BLOCK * DILATIONS[-1]
POOL_WINDOWS = (2, 4, 8, 16)
POOL_HALO = 16
REL_BUCKETS = 32
REL_MAX_DIST = 2048
MLSTM_HEADS = 8
MLSTM_CHUNK = 256
MLSTM_GROUP = 2
CONV_WIDTH = 4
CONV_HALO = 8
QKV_TILE = 128
VMEM_LIMIT = 56 * 1024 * 1024


def _params(*sem):
    return pltpu.CompilerParams(dimension_semantics=sem, vmem_limit_bytes=VMEM_LIMIT)


def _silu(x):
    return x * (1.0 / (1.0 + jnp.exp(-x)))


def _mm_body(*refs, n_a, has_res, has_scale, has_next, n_cols):
    a_refs = refs[:n_a]
    w_refs = refs[n_a:2 * n_a]
    p = 2 * n_a
    res_ref = refs[p] if has_res else None
    p += int(has_res)
    scale_ref = refs[p] if has_scale else None
    p += int(has_scale)
    gain_ref = refs[p] if has_next else None
    p += int(has_next)
    o_ref = refs[p]
    acc = None
    for a_ref, w_ref in zip(a_refs, w_refs):
        w = w_ref[...]
        d = jnp.dot(a_ref[...], w if w.dtype == BF16 else w.astype(BF16),
                    preferred_element_type=F32)
        acc = d if acc is None else acc + d
    if has_scale:
        acc = acc * scale_ref[:, 0:1]
    if has_res:
        acc = res_ref[...] + acc
    o_ref[...] = acc.astype(o_ref.dtype)
    if has_next:
        xg_ref, r_ref, ssq_ref = refs[p + 1], refs[p + 2], refs[p + 3]
        j = pl.program_id(1)
        xg_ref[...] = (acc * gain_ref[...]).astype(BF16)
        part = jnp.broadcast_to(jnp.sum(acc * acc, axis=-1, keepdims=True), ssq_ref.shape)

        @pl.when(j == 0)
        def _():
            ssq_ref[...] = part

        @pl.when(j > 0)
        def _():
            ssq_ref[...] += part

        @pl.when(j == pl.num_programs(1) - 1)
        def _():
            r_ref[...] = lax.rsqrt(ssq_ref[...] * (1.0 / n_cols) + NORM_EPS)


def _matmul(a_list, w_stack, layer, residual=None, row_scale=None, next_gain=None,
            out_dtype=F32, tm=1024, tn=512, a_buffers=1):
    M = a_list[0].shape[0]
    K, N = w_stack.shape[-2:]
    tm, tn = min(tm, M), min(tn, N)
    assert M % tm == 0 and N % tn == 0 and sum(a.shape[1] for a in a_list) == K
    in_specs, w_specs, row0 = [], [], 0
    for a in a_list:
        kp = a.shape[1]
        assert row0 % kp == 0
        in_specs.append(pl.BlockSpec((tm, kp), lambda i, j: (i, 0),
                                     pipeline_mode=pl.Buffered(a_buffers)))
        if w_stack.ndim == 3:
            w_specs.append(pl.BlockSpec((None, kp, tn), lambda i, j, rb=row0 // kp: (layer, rb, j)))
        else:
            w_specs.append(pl.BlockSpec((kp, tn), lambda i, j, rb=row0 // kp: (rb, j)))
        row0 += kp
    in_specs += w_specs
    args = list(a_list) + [w_stack] * len(a_list)
    tile = pl.BlockSpec((tm, tn), lambda i, j: (i, j))
    rows = pl.BlockSpec((tm, LANES), lambda i, j: (i, 0))
    if residual is not None:
        in_specs.append(tile)
        args.append(residual)
    if row_scale is not None:
        in_specs.append(rows)
        args.append(row_scale)
    out_specs, out_shape, scratch = tile, jax.ShapeDtypeStruct((M, N), out_dtype), []
    if next_gain is not None:
        in_specs.append(pl.BlockSpec((1, tn), lambda i, j: (0, j)))
        args.append(next_gain.reshape(1, N).astype(F32))
        out_specs = [tile, tile, rows]
        out_shape = [out_shape, jax.ShapeDtypeStruct((M, N), BF16), jax.ShapeDtypeStruct((M, LANES), F32)]
        scratch = [pltpu.VMEM((tm, LANES), F32)]
    return pl.pallas_call(
        functools.partial(_mm_body, n_a=len(a_list), has_res=residual is not None,
                          has_scale=row_scale is not None, has_next=next_gain is not None, n_cols=N),
        grid=(M // tm, N // tn),
        in_specs=in_specs,
        out_specs=out_specs,
        out_shape=out_shape,
        scratch_shapes=scratch,
        compiler_params=_params("arbitrary", "arbitrary"),
        name="matmul",
    )(*args)


def _side_cast_specs(w_stack, layer, n_steps, step_of):
    K, N = w_stack.shape[1:]
    n_blocks = 1 << (n_steps.bit_length() - 1)
    while K % n_blocks or (K // n_blocks) % 16:
        n_blocks //= 2
    rb = K // n_blocks
    idx = lambda *g: jnp.minimum(step_of(*g), n_blocks - 1)
    return (pl.BlockSpec((None, rb, N), lambda *g: (layer, idx(*g), 0)),
            pl.BlockSpec((rb, N), lambda *g: (idx(*g), 0)), n_blocks)


def _side_cast(wf_ref, wb_ref, step, n_blocks):
    @pl.when(step < n_blocks)
    def _():
        wb_ref[...] = wf_ref[...].astype(BF16)


def _norm_factors_body(x_ref, g_ref, xg_ref, r_ref):
    x = x_ref[...]
    xg_ref[...] = (x * g_ref[...]).astype(BF16)
    ms = jnp.mean(x * x, axis=-1, keepdims=True)
    r_ref[...] = jnp.broadcast_to(lax.rsqrt(ms + NORM_EPS), r_ref.shape)


def _norm_factors(x, gain, tm=256):
    M, D = x.shape
    tm = min(tm, M)
    return pl.pallas_call(
        _norm_factors_body,
        grid=(M // tm,),
        in_specs=[pl.BlockSpec((tm, D), lambda i: (i, 0)),
                  pl.BlockSpec((1, D), lambda i: (0, 0))],
        out_specs=[pl.BlockSpec((tm, D), lambda i: (i, 0)),
                   pl.BlockSpec((tm, LANES), lambda i: (i, 0))],
        out_shape=[jax.ShapeDtypeStruct((M, D), BF16), jax.ShapeDtypeStruct((M, LANES), F32)],
        compiler_params=_params("parallel"),
        name="norm_factors",
    )(x, gain.reshape(1, D).astype(F32))


def _t5_bucket(dist):
    max_exact = REL_BUCKETS // 2
    safe = np.maximum(dist, 1).astype(np.float32)
    large = max_exact + (np.log(safe / max_exact) / np.log(REL_MAX_DIST / max_exact)
                         * (REL_BUCKETS - max_exact)).astype(np.int32)
    large = np.minimum(large, REL_BUCKETS - 1)
    return np.where(dist < max_exact, dist, large).astype(np.int32)


def _attn_bucket_table():
    B = ATTN_BLOCK
    rel = B - np.arange(2 * B)
    band = (rel >= 0) & (rel <= B)
    rows = [np.where(band, _t5_bucket(np.clip(rel, 0, None) * d), -1) for d in DILATIONS]
    return np.broadcast_to(np.stack(rows)[:, None, :], (len(DILATIONS), 8, 2 * B)).astype(np.int32)


def _attn_body(relb_ref, bkt_ref, q_ref, k_ref, v_ref, z_ref, qg_ref, kg_ref, wf_ref,
               o_ref, wb_ref, bias_ref, qn_ref, q4_ref, kn_ref, k4_ref, vn_ref, v4_ref,
               o1_ref, m1_ref, l1_ref, o2_ref, m2_ref, l2_ref, o3_ref, m3_ref, l3_ref, out_ref,
               *, cast_blocks):
    B = ATTN_BLOCK
    C = ATTN_CHUNK
    R4 = DILATIONS[1]
    Q = C // R4
    h = pl.program_id(0)
    c = pl.program_id(1)
    n_dil = len(DILATIONS)
    cur = c % 2
    prv = 1 - cur

    @pl.when(c == 0)
    def _():
        col = lax.broadcasted_iota(jnp.int32, (B, 2 * B), 1)
        for t in range(n_dil):
            bkt = bkt_ref[t]
            row = jnp.full(bkt.shape, -jnp.inf, F32)
            for b in range(REL_BUCKETS):
                row = jnp.where(bkt == b, relb_ref[b, h] * LOG2E, row)
            bias = pltpu.roll(jnp.broadcast_to(row[0:1, :], (B, 2 * B)), 0, 1,
                              stride=1, stride_axis=0)
            bias_ref[t] = bias
            bias_ref[t + n_dil] = jnp.where(col >= B, bias, -jnp.inf)

    @pl.when(jnp.logical_and(h == 0, c == 0))
    def _():
        kn_ref[1] = jnp.zeros(kn_ref.shape[1:], F32)
        vn_ref[1] = jnp.zeros(vn_ref.shape[1:], F32)
        k4_ref[1] = jnp.zeros(k4_ref.shape[1:], F32)
        v4_ref[1] = jnp.zeros(v4_ref.shape[1:], F32)

    def _norm(x, g):
        ms = jnp.mean(x * x, axis=-1, keepdims=True)
        return x * lax.rsqrt(ms + NORM_EPS) * g

    qn_ref[...] = _norm(q_ref[...], qg_ref[...]) * (ATTN_HEAD_DIM ** -0.5 * LOG2E)
    kn_ref[cur] = _norm(k_ref[...], kg_ref[...])
    vn_ref[cur] = v_ref[...]
    for r4 in range(R4):
        rows = pl.ds(r4, Q, stride=R4)
        q4_ref[r4] = qn_ref[rows, :]
        k4_ref[cur, r4] = kn_ref[cur, rows, :]
        v4_ref[cur, r4] = vn_ref[cur, rows, :]

    first_chunk = jnp.where(c == 0, 1, 0)

    def block(q, k, v, bias):
        s = lax.dot_general(q.astype(BF16), k.astype(BF16), (((1,), (1,)), ((), ())),
                            preferred_element_type=F32) + bias
        m = jnp.max(s, axis=-1, keepdims=True)
        p = jnp.exp2(s - m).astype(BF16)
        v_aug = jnp.concatenate([v.astype(BF16), jnp.ones((2 * B, LANES), BF16)], axis=1)
        o_aug = jnp.dot(p, v_aug, preferred_element_type=F32)
        return o_aug[:, :ATTN_HEAD_DIM], m, o_aug[:, ATTN_HEAD_DIM:]

    def bcast(x):
        return jnp.broadcast_to(x, (B, LANES))

    for b in range(C // B):
        rb = slice(b * B, (b + 1) * B)
        if b == 0:
            ka, va = kn_ref[prv, C - B:C, :], vn_ref[prv, C - B:C, :]
            bias = bias_ref[n_dil * first_chunk]
        else:
            ka, va = kn_ref[cur, (b - 1) * B:b * B, :], vn_ref[cur, (b - 1) * B:b * B, :]
            bias = bias_ref[0]
        k = jnp.concatenate([ka, kn_ref[cur, rb, :]], axis=0)
        v = jnp.concatenate([va, vn_ref[cur, rb, :]], axis=0)
        o, m, l = block(qn_ref[rb, :], k, v, bias)
        o1_ref[rb, :] = o
        m1_ref[rb, :] = bcast(m)
        l1_ref[rb, :] = l

    for sub in range(Q // B):
        rb = slice(sub * B, (sub + 1) * B)
        bias = bias_ref[1 + n_dil * first_chunk] if sub == 0 else bias_ref[1]
        for r4 in range(R4):
            if sub == 0:
                ka, va = k4_ref[prv, r4, Q - B:Q, :], v4_ref[prv, r4, Q - B:Q, :]
            else:
                ra = slice((sub - 1) * B, sub * B)
                ka, va = k4_ref[cur, r4, ra, :], v4_ref[cur, r4, ra, :]
            k = jnp.concatenate([ka, k4_ref[cur, r4, rb, :]], axis=0)
            v = jnp.concatenate([va, v4_ref[cur, r4, rb, :]], axis=0)
            o, m, l = block(q4_ref[r4, rb, :], k, v, bias)
            o2_ref[r4, rb, :] = o
            m2_ref[r4, rb, :] = bcast(m)
            l2_ref[r4, rb, :] = l

    bias = bias_ref[2 + n_dil * first_chunk]
    for o4 in range(Q // B):
        rows = pl.ds(o4, B, stride=R4)
        for r4 in range(R4):
            k = jnp.concatenate([k4_ref[prv, r4, rows, :], k4_ref[cur, r4, rows, :]], axis=0)
            v = jnp.concatenate([v4_ref[prv, r4, rows, :], v4_ref[cur, r4, rows, :]], axis=0)
            o, m, l = block(q4_ref[r4, rows, :], k, v, bias)
            o3_ref[r4, rows, :] = o
            m3_ref[r4, rows, :] = bcast(m)
            l3_ref[r4, rows, :] = l

    def body_merge(sub, carry):
        r0 = pl.multiple_of(sub * B, B)
        for r4 in range(R4):
            nat = pl.ds(sub * (B * R4) + r4, B, stride=R4)
            m1, m2, m3 = m1_ref[nat, :], m2_ref[r4, pl.ds(r0, B), :], m3_ref[r4, pl.ds(r0, B), :]
            mx = jnp.maximum(jnp.maximum(m1, m2), m3)
            w1, w2, w3 = jnp.exp2(m1 - mx), jnp.exp2(m2 - mx), jnp.exp2(m3 - mx)
            num = (o1_ref[nat, :] * w1 + o2_ref[r4, pl.ds(r0, B), :] * w2
                   + o3_ref[r4, pl.ds(r0, B), :] * w3)
            den = (l1_ref[nat, :] * w1 + l2_ref[r4, pl.ds(r0, B), :] * w2
                   + l3_ref[r4, pl.ds(r0, B), :] * w3)
            out_ref[nat, :] = num / den
        return carry

    lax.fori_loop(0, Q // B, body_merge, 0)

    o_ref[...] = (out_ref[...] * _silu(z_ref[...])).astype(o_ref.dtype)
    _side_cast(wf_ref, wb_ref, h * pl.num_programs(1) + c, cast_blocks)


def _attention(proj, rel_bias, q_gain, k_gain, n_heads, z_col, w_stack, layer):
    S = proj.shape[0]
    C, B, Dh = ATTN_CHUNK, ATTN_BLOCK, ATTN_HEAD_DIM
    assert S % C == 0
    H = n_heads
    NC = S // C
    w_in_spec, w_out_spec, cast_blocks = _side_cast_specs(w_stack, layer, H * NC,
                                                          lambda h, c: h * NC + c)
    bkt = jnp.asarray(_attn_bucket_table())
    R4 = DILATIONS[1]
    blk = lambda f: pl.BlockSpec((C, Dh), f)
    nat = pltpu.VMEM((C, Dh), F32)
    mod4 = pltpu.VMEM((R4, C // R4, Dh), F32)
    return pl.pallas_call(
        functools.partial(_attn_body, cast_blocks=cast_blocks),
        grid=(H, NC),
        in_specs=[
            pl.BlockSpec(memory_space=pltpu.SMEM),
            pl.BlockSpec(bkt.shape, lambda h, c: (0, 0, 0)),
            blk(lambda h, c: (c, h)),
            blk(lambda h, c: (c, H + h)),
            blk(lambda h, c: (c, 2 * H + h)),
            blk(lambda h, c: (c, z_col + h)),
            pl.BlockSpec((1, Dh), lambda h, c: (0, 0)),
            pl.BlockSpec((1, Dh), lambda h, c: (0, 0)),
            w_in_spec,
        ],
        out_specs=[blk(lambda h, c: (c, h)), w_out_spec],
        out_shape=[jax.ShapeDtypeStruct((S, H * Dh), BF16),
                   jax.ShapeDtypeStruct(w_stack.shape[1:], BF16)],
        scratch_shapes=[
            pltpu.VMEM((2 * len(DILATIONS), B, 2 * B), F32),
            nat, mod4,
            pltpu.VMEM((2, C, Dh), F32), pltpu.VMEM((2, R4, C // R4, Dh), F32),
            pltpu.VMEM((2, C, Dh), F32), pltpu.VMEM((2, R4, C // R4, Dh), F32),
            nat, nat, nat,
            mod4, mod4, mod4,
            mod4, mod4, mod4,
            nat,
        ],
        compiler_params=_params("arbitrary", "arbitrary"),
        name="dilated_attention",
    )(rel_bias.astype(F32), bkt, proj, proj, proj, proj,
      q_gain.reshape(1, Dh).astype(F32), k_gain.reshape(1, Dh).astype(F32), w_stack)


def _pool_body(u_ref, halo_ref, z_ref, w_ref, sc_ref, o_ref, *, group_dim):
    T = u_ref.shape[0]
    i = pl.program_id(0)
    halo_on = jnp.where(i > 0, 1.0, 0.0)
    pos = (i * T + lax.broadcasted_iota(jnp.int32, (T, 1), 0) + 1).astype(F32)
    for g, w in enumerate(POOL_WINDOWS):
        cols = slice(g * group_dim, (g + 1) * group_dim)
        x = u_ref[:, cols]
        e = jnp.concatenate([halo_ref[:, cols] * halo_on, x], axis=0)
        width = 1
        while width < w:
            e = e[width:, :] + e[:-width, :]
            width *= 2
        off = POOL_HALO - (w - 1)
        win = e[off:off + T, :]
        y = win / jnp.minimum(pos, float(w)) - x
        yp = jnp.dot(y.astype(BF16), w_ref[g].astype(BF16), preferred_element_type=F32)
        o_ref[:, cols] = (yp * sc_ref[:, cols] * _silu(z_ref[:, cols])).astype(o_ref.dtype)


def _pool(proj, pool_w, pool_scale, u_col, z_col, tile=256):
    S = proj.shape[0]
    G, Cg, _ = pool_w.shape
    P = G * Cg
    T = min(tile, S)
    assert u_col % P == 0 and z_col % P == 0 and T % POOL_HALO == 0
    return pl.pallas_call(
        functools.partial(_pool_body, group_dim=Cg),
        grid=(S // T,),
        in_specs=[
            pl.BlockSpec((T, P), lambda i: (i, u_col // P)),
            pl.BlockSpec((POOL_HALO, P), lambda i: (jnp.maximum(i * (T // POOL_HALO) - 1, 0), u_col // P)),
            pl.BlockSpec((T, P), lambda i: (i, z_col // P)),
            pl.BlockSpec((G, Cg, Cg), lambda i: (0, 0, 0)),
            pl.BlockSpec((1, P), lambda i: (0, 0)),
        ],
        out_specs=pl.BlockSpec((T, P), lambda i: (i, 0)),
        out_shape=jax.ShapeDtypeStruct((S, P), BF16),
        compiler_params=_params("parallel"),
        name="multiscale_pool",
    )(proj, proj, proj, pool_w.astype(F32), pool_scale.reshape(1, P).astype(F32))


def _front_body(xm_ref, halo_ref, cw_ref, cb_ref, wq_ref, wk_ref, wv_ref, wif_ref, bif_ref,
                xc_ref, q_ref, k_ref, v_ref, g_ref):
    T, TC = xm_ref.shape
    i = pl.program_id(0)
    j = pl.program_id(1)
    xm = xm_ref[...]
    halo = halo_ref[...] * jnp.where(i > 0, 1.0, 0.0)
    e = jnp.concatenate([halo, xm], axis=0)
    conv = cb_ref[...]
    for t in reversed(range(CONV_WIDTH)):
        off = CONV_HALO - (CONV_WIDTH - 1) + t
        conv = conv + e[off:off + T, :] * cw_ref[t:t + 1, :]
    xcb = _silu(conv).astype(BF16)
    xc_ref[...] = xcb
    xmb = xm.astype(BF16)
    gates = jnp.zeros(g_ref.shape, F32)
    W = wq_ref.shape[-1]
    for n in range(TC // W):
        cols = slice(n * W, (n + 1) * W)
        wqk = jnp.concatenate([wq_ref[n], wk_ref[n]], axis=1)
        yqk = jnp.dot(xcb[:, cols], wqk, preferred_element_type=F32).astype(BF16)
        q_ref[:, cols] = yqk[:, :W]
        k_ref[:, cols] = yqk[:, W:]
        wif_qk = jnp.concatenate([wif_ref[0, cols, :], wif_ref[1, cols, :]], axis=0)
        yv = jnp.dot(xmb[:, cols], wv_ref[n], preferred_element_type=F32).astype(BF16)
        v_ref[:, cols] = yv
        gates = (gates + jnp.dot(yqk, wif_qk, preferred_element_type=F32)
                 + jnp.dot(yv, wif_ref[2, cols, :], preferred_element_type=F32))

    @pl.when(j == 0)
    def _():
        g_ref[...] = bif_ref[...] + gates

    @pl.when(j > 0)
    def _():
        g_ref[...] += gates


def _block_diag_dense(w, width):
    nb, bs, _ = w.shape
    per = width // bs
    wg = w.reshape(nb // per, per, bs, bs)
    eye = jnp.eye(per, dtype=w.dtype)
    dense = jnp.einsum('gpcd,pq->gpcqd', wg, eye)
    return dense.reshape(nb // per, width, width).astype(BF16)


def _mlstm_front(up, conv_w, conv_b, wq, wk, wv, w_if, b_if, tile=512, tcol=1024):
    S = up.shape[0]
    E = conv_w.shape[1]
    T, TC = min(tile, S), min(tcol, E)
    NG = w_if.shape[1]
    W = min(QKV_TILE, TC)
    nt = TC // W
    wdense = [_block_diag_dense(w, W) for w in (wq, wk, wv)]
    wif = w_if.reshape(3, E, NG).astype(BF16)
    row = lambda: pl.BlockSpec((T, TC), lambda i, j: (i, j))
    return pl.pallas_call(
        _front_body,
        grid=(S // T, E // TC),
        in_specs=[
            row(),
            pl.BlockSpec((CONV_HALO, TC), lambda i, j: (jnp.maximum(i * (T // CONV_HALO) - 1, 0), j)),
            pl.BlockSpec((CONV_WIDTH, TC), lambda i, j: (0, j)),
            pl.BlockSpec((1, TC), lambda i, j: (0, j)),
            pl.BlockSpec((nt, W, W), lambda i, j: (j, 0, 0)),
            pl.BlockSpec((nt, W, W), lambda i, j: (j, 0, 0)),
            pl.BlockSpec((nt, W, W), lambda i, j: (j, 0, 0)),
            pl.BlockSpec((3, TC, NG), lambda i, j: (0, j, 0)),
            pl.BlockSpec((1, NG), lambda i, j: (0, 0)),
        ],
        out_specs=[row(), row(), row(), row(), pl.BlockSpec((T, NG), lambda i, j: (i, 0))],
        out_shape=[jax.ShapeDtypeStruct((S, E), BF16)] * 4 + [jax.ShapeDtypeStruct((S, NG), F32)],
        compiler_params=_params("parallel", "arbitrary"),
        name="mlstm_front",
    )(up, up, conv_w.astype(F32), conv_b.reshape(1, E).astype(F32), *wdense, wif,
      b_if.reshape(1, NG).astype(F32))


def _mlstm_body(q_ref, k_ref, v_ref, ig_ref, fg_ref, op_ref, xc_ref, z_ref, gn_ref, sk_ref,
                wf_ref, o_ref, wb_ref, c_ref, cb_ref, n_ref, m_ref, *, cast_blocks):
    L = q_ref.shape[0]
    G, DK, _ = c_ref.shape
    c = pl.program_id(1)
    _side_cast(wf_ref, wb_ref, pl.program_id(0) * pl.num_programs(1) + c, cast_blocks)

    @pl.when(c == 0)
    def _():
        c_ref[...] = jnp.zeros_like(c_ref)
        cb_ref[...] = jnp.zeros_like(cb_ref)
        n_ref[...] = jnp.zeros_like(n_ref)
        m_ref[...] = jnp.full(m_ref.shape, -1e30, F32)

    ri = lax.broadcasted_iota(jnp.int32, (L, L), 0)
    cj = lax.broadcasted_iota(jnp.int32, (L, L), 1)
    lane8 = lax.broadcasted_iota(jnp.int32, (8, L), 1)
    scale = DK ** -0.5

    for hh in range(G):
        cols = slice(hh * DK, (hh + 1) * DK)
        i_row = ig_ref[hh, 0]
        f_row = fg_ref[hh, 0]
        lf_row = jnp.minimum(f_row, 0.0) - jnp.log1p(jnp.exp(-jnp.abs(f_row)))
        b8 = jnp.broadcast_to(lf_row, (8, L))
        sh = 1
        while sh < L:
            b8 = b8 + jnp.where(lane8 >= sh, pltpu.roll(b8, sh, 1), 0.0)
            sh *= 2
        b_row = b8[0:1, :]
        stacked = jnp.where(ri == 0, jnp.broadcast_to(b_row, (L, L)),
                            jnp.where(ri == 1, jnp.broadcast_to(i_row, (L, L)), 0.0))
        stacked_t = stacked.T
        b_col = stacked_t[:, 0:1]
        i_col = stacked_t[:, 1:2]

        m_prev = m_ref[hh, 0:1, 0:1]
        log_d = jnp.where(cj <= ri, b_col - b_row + i_row, -jnp.inf)
        log_inter = b_col + m_prev
        m_t = jnp.maximum(jnp.max(log_d, axis=-1, keepdims=True), log_inter)
        dmat = jnp.exp(log_d - m_t) * scale
        g = jnp.exp(log_inter - m_t)

        q = q_ref[:, cols]
        k = k_ref[:, cols]
        v = v_ref[:, cols]
        s = lax.dot_general(q, k, (((1,), (1,)), ((), ())), preferred_element_type=F32) * dmat
        inter = jnp.dot(q, cb_ref[hh], preferred_element_type=F32)
        num = jnp.dot(s.astype(BF16), v, preferred_element_type=F32) + g * inter
        qn = jnp.sum(q.astype(F32) * n_ref[hh], axis=-1, keepdims=True)
        den = jnp.sum(s, axis=-1, keepdims=True) + g * qn
        hc = num / jnp.maximum(jnp.abs(den), jnp.exp(-m_t))

        m_new = m_t[L - 1:L, :]
        b_last = b_col[L - 1:L, :]
        decay = jnp.exp(b_last + m_prev - m_new)
        w_col = jnp.exp(b_last - b_col + i_col - m_new) * scale
        vw = (v.astype(F32) * w_col).astype(BF16)
        upd = lax.dot_general(k, vw, (((0,), (0,)), ((), ())), preferred_element_type=F32)
        c_new = upd + c_ref[hh] * decay
        c_ref[hh] = c_new
        cb_ref[hh] = c_new.astype(BF16)
        n_ref[hh] = n_ref[hh] * decay + jnp.sum(k.astype(F32) * w_col, axis=0, keepdims=True)
        m_ref[hh] = jnp.broadcast_to(m_new, m_ref.shape[1:])

        mu = jnp.mean(hc, axis=-1, keepdims=True)
        ctr = hc - mu
        var = jnp.mean(ctr * ctr, axis=-1, keepdims=True)
        hn = ctr * lax.rsqrt(var + NORM_EPS)
        cell = (1.0 / (1.0 + jnp.exp(-op_ref[:, cols]))) * (hn * gn_ref[:, cols])
        o_ref[:, cols] = ((cell + sk_ref[:, cols] * xc_ref[:, cols].astype(F32))
                          * _silu(z_ref[:, cols])).astype(o_ref.dtype)


def _mlstm(q, k, v, gates, up, xc, gn, skip, w_stack, layer):
    S, E = q.shape
    H, L, G = MLSTM_HEADS, MLSTM_CHUNK, MLSTM_GROUP
    DH = E // H
    NCH = S // L
    HG = H // G
    w_in_spec, w_out_spec, cast_blocks = _side_cast_specs(w_stack, layer, HG * NCH,
                                                          lambda h, c: h * NCH + c)
    gt = gates.T.reshape(2 * H, NCH, 1, L)
    blk = lambda col0: pl.BlockSpec((L, G * DH), lambda h, c: (c, col0 + h))
    vec = pl.BlockSpec((1, G * DH), lambda h, c: (0, h))
    return pl.pallas_call(
        functools.partial(_mlstm_body, cast_blocks=cast_blocks),
        grid=(HG, NCH),
        in_specs=[
            blk(0), blk(0), blk(0),
            pl.BlockSpec((G, 1, 1, L), lambda h, c: (h, c, 0, 0)),
            pl.BlockSpec((G, 1, 1, L), lambda h, c: (HG + h, c, 0, 0)),
            blk(2 * HG), blk(0), blk(HG),
            vec, vec,
            w_in_spec,
        ],
        out_specs=[blk(0), w_out_spec],
        out_shape=[jax.ShapeDtypeStruct((S, E), BF16), jax.ShapeDtypeStruct(w_stack.shape[1:], BF16)],
        scratch_shapes=[
            pltpu.VMEM((G, DH, DH), F32),
            pltpu.VMEM((G, DH, DH), BF16),
            pltpu.VMEM((G, 1, DH), F32),
            pltpu.VMEM((G, 8, LANES), F32),
        ],
        compiler_params=_params("arbitrary", "arbitrary"),
        name="mlstm_chunkwise",
    )(q, k, v, gt, gt, up, xc, up, gn.reshape(1, E).astype(F32), skip.reshape(1, E).astype(F32),
      w_stack)


def _even_layer(h, xg, r, j, rel_bias, w_in, q_gain, k_gain, pool_w, pool_scale, w_out, next_gain):
    mix = w_out.shape[1]
    pool_width = pool_w.shape[0] * pool_w.shape[1]
    attn_width = mix - pool_width
    n_heads = attn_width // ATTN_HEAD_DIM
    u_col = 3 * attn_width
    z_col = u_col + pool_width
    proj = _matmul([xg], w_in, j, row_scale=r, tm=IN_PROJ_ROWS)
    attn, w_out_bf16 = _attention(proj, rel_bias, q_gain, k_gain, n_heads,
                                  z_col // ATTN_HEAD_DIM, w_out, j)
    pool = _pool(proj, pool_w, pool_scale, u_col, z_col + attn_width)
    return _matmul([attn, pool], w_out_bf16, j, residual=h, next_gain=next_gain,
                   tn=OUT_PROJ_COLS, a_buffers=2)


def _odd_layer(h, xg, r, j, w_up, conv_w, conv_b, wq, wk, wv, w_if, b_if, gn, skip, w_down,
               next_gain):
    up = _matmul([xg], w_up, j, row_scale=r, tm=IN_PROJ_ROWS)
    xc, q, k, v, gates = _mlstm_front(up, conv_w, conv_b, wq, wk, wv, w_if, b_if)
    out, w_down_bf16 = _mlstm(q, k, v, gates, up, xc, gn, skip, w_down, j)
    return _matmul([out], w_down_bf16, j, residual=h, next_gain=next_gain,
                   tn=OUT_PROJ_COLS, a_buffers=2)


def kernel(x, rel_bias, e_norm, e_w_in, e_q_gain, e_k_gain, e_pool_w, e_pool_scale, e_w_out,
           o_norm, o_w_up, o_conv_w, o_conv_b, o_wq, o_wk, o_wv, o_w_if, o_b_if, o_gn, o_skip,
           o_w_down):
    B, S, D = x.shape
    depth = e_norm.shape[0] + o_norm.shape[0]
    outs = []
    for b in range(B):
        h = x[b]
        gains = [(e_norm if layer % 2 == 0 else o_norm)[layer // 2] for layer in range(depth)]
        xg, r = _norm_factors(h, gains[0])
        for layer in range(depth):
            j = layer // 2
            next_gain = gains[layer + 1] if layer + 1 < depth else None
            if layer % 2 == 0:
                res = _even_layer(h, xg, r, j, rel_bias, e_w_in, e_q_gain[j], e_k_gain[j],
                                  e_pool_w[j], e_pool_scale[j], e_w_out, next_gain)
            else:
                res = _odd_layer(h, xg, r, j, o_w_up, o_conv_w[j], o_conv_b[j], o_wq[j], o_wk[j],
                                 o_wv[j], o_w_if[j], o_b_if[j], o_gn[j], o_skip[j], o_w_down,
                                 next_gain)
            h, xg, r = res if next_gain is not None else (res, None, None)
        outs.append(h)
    return jnp.stack(outs)
```

```python
import functools

import numpy as np
import jax
import jax.numpy as jnp
from jax import lax
from jax.experimental import pallas as pl
from jax.experimental.pallas import tpu as pltpu

F32 = jnp.float32
BF16 = jnp.bfloat16

NORM_EPS = 1e-6
IN_PROJ_ROWS = 2048
OUT_PROJ_COLS = 256
LOG2E = 1.4426950408889634
LANES = 128
ATTN_HEAD_DIM = 128
ATTN_BLOCK = 128
DILATIONS = (1, 4, 16)
ATTN_CHUNK = ATTN_BLOCK * DILATIONS[-1]
POOL_WINDOWS = (2, 4, 8, 16)
POOL_HALO = 16
REL_BUCKETS = 32
REL_MAX_DIST = 2048
MLSTM_HEADS = 8
MLSTM_CHUNK = 256
MLSTM_GROUP = 2
CONV_WIDTH = 4
CONV_HALO = 8
QKV_TILE = 128
VMEM_LIMIT = 56 * 1024 * 1024


def _params(*sem):
    return pltpu.CompilerParams(dimension_semantics=sem, vmem_limit_bytes=VMEM_LIMIT)


def _silu(x):
    return x * (1.0 / (1.0 + jnp.exp(-x)))


def _mm_body(*refs, n_a, has_res, has_scale, has_next, n_cols):
    a_refs = refs[:n_a]
    w_refs = refs[n_a:2 * n_a]
    p = 2 * n_a
    res_ref = refs[p] if has_res else None
    p += int(has_res)
    scale_ref = refs[p] if has_scale else None
    p += int(has_scale)
    gain_ref = refs[p] if has_next else None
    p += int(has_next)
    o_ref = refs[p]
    acc = None
    for a_ref, w_ref in zip(a_refs, w_refs):
        w = w_ref[...]
        d = jnp.dot(a_ref[...], w if w.dtype == BF16 else w.astype(BF16),
                    preferred_element_type=F32)
        acc = d if acc is None else acc + d
    if has_scale:
        acc = acc * scale_ref[:, 0:1]
    if has_res:
        acc = res_ref[...] + acc
    o_ref[...] = acc.astype(o_ref.dtype)
    if has_next:
        xg_ref, r_ref, ssq_ref = refs[p + 1], refs[p + 2], refs[p + 3]
        j = pl.program_id(1)
        xg_ref[...] = (acc * gain_ref[...]).astype(BF16)
        part = jnp.broadcast_to(jnp.sum(acc * acc, axis=-1, keepdims=True), ssq_ref.shape)

        @pl.when(j == 0)
        def _():
            ssq_ref[...] = part

        @pl.when(j > 0)
        def _():
            ssq_ref[...] += part

        @pl.when(j == pl.num_programs(1) - 1)
        def _():
            r_ref[...] = lax.rsqrt(ssq_ref[...] * (1.0 / n_cols) + NORM_EPS)


def _matmul(a_list, w_stack, layer, residual=None, row_scale=None, next_gain=None,
            out_dtype=F32, tm=1024, tn=512, a_buffers=1):
    M = a_list[0].shape[0]
    K, N = w_stack.shape[-2:]
    tm, tn = min(tm, M), min(tn, N)
    assert M % tm == 0 and N % tn == 0 and sum(a.shape[1] for a in a_list) == K
    in_specs, w_specs, row0 = [], [], 0
    for a in a_list:
        kp = a.shape[1]
        assert row0 % kp == 0
        in_specs.append(pl.BlockSpec((tm, kp), lambda i, j: (i, 0),
                                     pipeline_mode=pl.Buffered(a_buffers)))
        if w_stack.ndim == 3:
            w_specs.append(pl.BlockSpec((None, kp, tn), lambda i, j, rb=row0 // kp: (layer, rb, j)))
        else:
            w_specs.append(pl.BlockSpec((kp, tn), lambda i, j, rb=row0 // kp: (rb, j)))
        row0 += kp
    in_specs += w_specs
    args = list(a_list) + [w_stack] * len(a_list)
    tile = pl.BlockSpec((tm, tn), lambda i, j: (i, j))
    rows = pl.BlockSpec((tm, LANES), lambda i, j: (i, 0))
    if residual is not None:
        in_specs.append(tile)
        args.append(residual)
    if row_scale is not None:
        in_specs.append(rows)
        args.append(row_scale)
    out_specs, out_shape, scratch = tile, jax.ShapeDtypeStruct((M, N), out_dtype), []
    if next_gain is not None:
        in_specs.append(pl.BlockSpec((1, tn), lambda i, j: (0, j)))
        args.append(next_gain.reshape(1, N).astype(F32))
        out_specs = [tile, tile, rows]
        out_shape = [out_shape, jax.ShapeDtypeStruct((M, N), BF16), jax.ShapeDtypeStruct((M, LANES), F32)]
        scratch = [pltpu.VMEM((tm, LANES), F32)]
    return pl.pallas_call(
        functools.partial(_mm_body, n_a=len(a_list), has_res=residual is not None,
                          has_scale=row_scale is not None, has_next=next_gain is not None, n_cols=N),
        grid=(M // tm, N // tn),
        in_specs=in_specs,
        out_specs=out_specs,
        out_shape=out_shape,
        scratch_shapes=scratch,
        compiler_params=_params("arbitrary", "arbitrary"),
        name="matmul",
    )(*args)


def _side_cast_specs(w_stack, layer, n_steps, step_of):
    K, N = w_stack.shape[1:]
    n_blocks = 1 << (n_steps.bit_length() - 1)
    while K % n_blocks or (K // n_blocks) % 16:
        n_blocks //= 2
    rb = K // n_blocks
    idx = lambda *g: jnp.minimum(step_of(*g), n_blocks - 1)
    return (pl.BlockSpec((None, rb, N), lambda *g: (layer, idx(*g), 0)),
            pl.BlockSpec((rb, N), lambda *g: (idx(*g), 0)), n_blocks)


def _side_cast(wf_ref, wb_ref, step, n_blocks):
    @pl.when(step < n_blocks)
    def _():
        wb_ref[...] = wf_ref[...].astype(BF16)


def _norm_factors_body(x_ref, g_ref, xg_ref, r_ref):
    x = x_ref[...]
    xg_ref[...] = (x * g_ref[...]).astype(BF16)
    ms = jnp.mean(x * x, axis=-1, keepdims=True)
    r_ref[...] = jnp.broadcast_to(lax.rsqrt(ms + NORM_EPS), r_ref.shape)


def _norm_factors(x, gain, tm=256):
    M, D = x.shape
    tm = min(tm, M)
    return pl.pallas_call(
        _norm_factors_body,
        grid=(M // tm,),
        in_specs=[pl.BlockSpec((tm, D), lambda i: (i, 0)),
                  pl.BlockSpec((1, D), lambda i: (0, 0))],
        out_specs=[pl.BlockSpec((tm, D), lambda i: (i, 0)),
                   pl.BlockSpec((tm, LANES), lambda i: (i, 0))],
        out_shape=[jax.ShapeDtypeStruct((M, D), BF16), jax.ShapeDtypeStruct((M, LANES), F32)],
        compiler_params=_params("parallel"),
        name="norm_factors",
    )(x, gain.reshape(1, D).astype(F32))


def _t5_bucket(dist):
    max_exact = REL_BUCKETS // 2
    safe = np.maximum(dist, 1).astype(np.float32)
    large = max_exact + (np.log(safe / max_exact) / np.log(REL_MAX_DIST / max_exact)
                         * (REL_BUCKETS - max_exact)).astype(np.int32)
    large = np.minimum(large, REL_BUCKETS - 1)
    return np.where(dist < max_exact, dist, large).astype(np.int32)


def _attn_bucket_table():
    B = ATTN_BLOCK
    rel = B - np.arange(2 * B)
    band = (rel >= 0) & (rel <= B)
    rows = [np.where(band, _t5_bucket(np.clip(rel, 0, None) * d), -1) for d in DILATIONS]
    return np.broadcast_to(np.stack(rows)[:, None, :], (len(DILATIONS), 8, 2 * B)).astype(np.int32)


def _attn_body(relb_ref, bkt_ref, q_ref, k_ref, v_ref, z_ref, qg_ref, kg_ref, wf_ref,
               o_ref, wb_ref, bias_ref, qn_ref, q4_ref, kn_ref, k4_ref, vn_ref, v4_ref,
               o1_ref, m1_ref, l1_ref, o2_ref, m2_ref, l2_ref, o3_ref, m3_ref, l3_ref, out_ref,
               *, cast_blocks):
    B = ATTN_BLOCK
    C = ATTN_CHUNK
    R4 = DILATIONS[1]
    Q = C // R4
    h = pl.program_id(0)
    c = pl.program_id(1)
    n_dil = len(DILATIONS)
    cur = c % 2
    prv = 1 - cur

    @pl.when(c == 0)
    def _():
        col = lax.broadcasted_iota(jnp.int32, (B, 2 * B), 1)
        for t in range(n_dil):
            bkt = bkt_ref[t]
            row = jnp.full(bkt.shape, -jnp.inf, F32)
            for b in range(REL_BUCKETS):
                row = jnp.where(bkt == b, relb_ref[b, h] * LOG2E, row)
            bias = pltpu.roll(jnp.broadcast_to(row[0:1, :], (B, 2 * B)), 0, 1,
                              stride=1, stride_axis=0)
            bias_ref[t] = bias
            bias_ref[t + n_dil] = jnp.where(col >= B, bias, -jnp.inf)

    @pl.when(jnp.logical_and(h == 0, c == 0))
    def _():
        kn_ref[1] = jnp.zeros(kn_ref.shape[1:], F32)
        vn_ref[1] = jnp.zeros(vn_ref.shape[1:], F32)
        k4_ref[1] = jnp.zeros(k4_ref.shape[1:], F32)
        v4_ref[1] = jnp.zeros(v4_ref.shape[1:], F32)

    def _norm(x, g):
        ms = jnp.mean(x * x, axis=-1, keepdims=True)
        return x * lax.rsqrt(ms + NORM_EPS) * g

    qn_ref[...] = _norm(q_ref[...], qg_ref[...]) * (ATTN_HEAD_DIM ** -0.5 * LOG2E)
    kn_ref[cur] = _norm(k_ref[...], kg_ref[...])
    vn_ref[cur] = v_ref[...]
    for r4 in range(R4):
        rows = pl.ds(r4, Q, stride=R4)
        q4_ref[r4] = qn_ref[rows, :]
        k4_ref[cur, r4] = kn_ref[cur, rows, :]
        v4_ref[cur, r4] = vn_ref[cur, rows, :]

    first_chunk = jnp.where(c == 0, 1, 0)

    def block(q, k, v, bias):
        s = lax.dot_general(q.astype(BF16), k.astype(BF16), (((1,), (1,)), ((), ())),
                            preferred_element_type=F32) + bias
        m = jnp.max(s, axis=-1, keepdims=True)
        p = jnp.exp2(s - m).astype(BF16)
        v_aug = jnp.concatenate([v.astype(BF16), jnp.ones((2 * B, LANES), BF16)], axis=1)
        o_aug = jnp.dot(p, v_aug, preferred_element_type=F32)
        return o_aug[:, :ATTN_HEAD_DIM], m, o_aug[:, ATTN_HEAD_DIM:]

    def bcast(x):
        return jnp.broadcast_to(x, (B, LANES))

    for b in range(C // B):
        rb = slice(b * B, (b + 1) * B)
        if b == 0:
            ka, va = kn_ref[prv, C - B:C, :], vn_ref[prv, C - B:C, :]
            bias = bias_ref[n_dil * first_chunk]
        else:
            ka, va = kn_ref[cur, (b - 1) * B:b * B, :], vn_ref[cur, (b - 1) * B:b * B, :]
            bias = bias_ref[0]
        k = jnp.concatenate([ka, kn_ref[cur, rb, :]], axis=0)
        v = jnp.concatenate([va, vn_ref[cur, rb, :]], axis=0)
        o, m, l = block(qn_ref[rb, :], k, v, bias)
        o1_ref[rb, :] = o
        m1_ref[rb, :] = bcast(m)
        l1_ref[rb, :] = l

    for sub in range(Q // B):
        rb = slice(sub * B, (sub + 1) * B)
        bias = bias_ref[1 + n_dil * first_chunk] if sub == 0 else bias_ref[1]
        for r4 in range(R4):
            if sub == 0:
                ka, va = k4_ref[prv, r4, Q - B:Q, :], v4_ref[prv, r4, Q - B:Q, :]
            else:
                ra = slice((sub - 1) * B, sub * B)
                ka, va = k4_ref[cur, r4, ra, :], v4_ref[cur, r4, ra, :]
            k = jnp.concatenate([ka, k4_ref[cur, r4, rb, :]], axis=0)
            v = jnp.concatenate([va, v4_ref[cur, r4, rb, :]], axis=0)
            o, m, l = block(q4_ref[r4, rb, :], k, v, bias)
            o2_ref[r4, rb, :] = o
            m2_ref[r4, rb, :] = bcast(m)
            l2_ref[r4, rb, :] = l

    bias = bias_ref[2 + n_dil * first_chunk]
    for o4 in range(Q // B):
        rows = pl.ds(o4, B, stride=R4)
        for r4 in range(R4):
            k = jnp.concatenate([k4_ref[prv, r4, rows, :], k4_ref[cur, r4, rows, :]], axis=0)
            v = jnp.concatenate([v4_ref[prv, r4, rows, :], v4_ref[cur, r4, rows, :]], axis=0)
            o, m, l = block(q4_ref[r4, rows, :], k, v, bias)
            o3_ref[r4, rows, :] = o
            m3_ref[r4, rows, :] = bcast(m)
            l3_ref[r4, rows, :] = l

    def body_merge(sub, carry):
        r0 = pl.multiple_of(sub * B, B)
        for r4 in range(R4):
            nat = pl.ds(sub * (B * R4) + r4, B, stride=R4)
            m1, m2, m3 = m1_ref[nat, :], m2_ref[r4, pl.ds(r0, B), :], m3_ref[r4, pl.ds(r0, B), :]
            mx = jnp.maximum(jnp.maximum(m1, m2), m3)
            w1, w2, w3 = jnp.exp2(m1 - mx), jnp.exp2(m2 - mx), jnp.exp2(m3 - mx)
            num = (o1_ref[nat, :] * w1 + o2_ref[r4, pl.ds(r0, B), :] * w2
                   + o3_ref[r4, pl.ds(r0, B), :] * w3)
            den = (l1_ref[nat, :] * w1 + l2_ref[r4, pl.ds(r0, B), :] * w2
                   + l3_ref[r4, pl.ds(r0, B), :] * w3)
            out_ref[nat, :] = num / den
        return carry

    lax.fori_loop(0, Q // B, body_merge, 0)

    o_ref[...] = (out_ref[...] * _silu(z_ref[...])).astype(o_ref.dtype)
    _side_cast(wf_ref, wb_ref, h * pl.num_programs(1) + c, cast_blocks)


def _attention(proj, rel_bias, q_gain, k_gain, n_heads, z_col, w_stack, layer):
    S = proj.shape[0]
    C, B, Dh = ATTN_CHUNK, ATTN_BLOCK, ATTN_HEAD_DIM
    assert S % C == 0
    H = n_heads
    NC = S // C
    w_in_spec, w_out_spec, cast_blocks = _side_cast_specs(w_stack, layer, H * NC,
                                                          lambda h, c: h * NC + c)
    bkt = jnp.asarray(_attn_bucket_table())
    R4 = DILATIONS[1]
    blk = lambda f: pl.BlockSpec((C, Dh), f)
    nat = pltpu.VMEM((C, Dh), F32)
    mod4 = pltpu.VMEM((R4, C // R4, Dh), F32)
    return pl.pallas_call(
        functools.partial(_attn_body, cast_blocks=cast_blocks),
        grid=(H, NC),
        in_specs=[
            pl.BlockSpec(memory_space=pltpu.SMEM),
            pl.BlockSpec(bkt.shape, lambda h, c: (0, 0, 0)),
            blk(lambda h, c: (c, h)),
            blk(lambda h, c: (c, H + h)),
            blk(lambda h, c: (c, 2 * H + h)),
            blk(lambda h, c: (c, z_col + h)),
            pl.BlockSpec((1, Dh), lambda h, c: (0, 0)),
            pl.BlockSpec((1, Dh), lambda h, c: (0, 0)),
            w_in_spec,
        ],
        out_specs=[blk(lambda h, c: (c, h)), w_out_spec],
        out_shape=[jax.ShapeDtypeStruct((S, H * Dh), BF16),
                   jax.ShapeDtypeStruct(w_stack.shape[1:], BF16)],
        scratch_shapes=[
            pltpu.VMEM((2 * len(DILATIONS), B, 2 * B), F32),
            nat, mod4,
            pltpu.VMEM((2, C, Dh), F32), pltpu.VMEM((2, R4, C // R4, Dh), F32),
            pltpu.VMEM((2, C, Dh), F32), pltpu.VMEM((2, R4, C // R4, Dh), F32),
            nat, nat, nat,
            mod4, mod4, mod4,
            mod4, mod4, mod4,
            nat,
        ],
        compiler_params=_params("arbitrary", "arbitrary"),
        name="dilated_attention",
    )(rel_bias.astype(F32), bkt, proj, proj, proj, proj,
      q_gain.reshape(1, Dh).astype(F32), k_gain.reshape(1, Dh).astype(F32), w_stack)


def _pool_body(u_ref, halo_ref, z_ref, w_ref, sc_ref, o_ref, *, group_dim):
    T = u_ref.shape[0]
    i = pl.program_id(0)
    halo_on = jnp.where(i > 0, 1.0, 0.0)
    pos = (i * T + lax.broadcasted_iota(jnp.int32, (T, 1), 0) + 1).astype(F32)
    for g, w in enumerate(POOL_WINDOWS):
        cols = slice(g * group_dim, (g + 1) * group_dim)
        x = u_ref[:, cols]
        e = jnp.concatenate([halo_ref[:, cols] * halo_on, x], axis=0)
        width = 1
        while width < w:
            e = e[width:, :] + e[:-width, :]
            width *= 2
        off = POOL_HALO - (w - 1)
        win = e[off:off + T, :]
        y = win / jnp.minimum(pos, float(w)) - x
        yp = jnp.dot(y.astype(BF16), w_ref[g].astype(BF16), preferred_element_type=F32)
        o_ref[:, cols] = (yp * sc_ref[:, cols] * _silu(z_ref[:, cols])).astype(o_ref.dtype)


def _pool(proj, pool_w, pool_scale, u_col, z_col, tile=256):
    S = proj.shape[0]
    G, Cg, _ = pool_w.shape
    P = G * Cg
    T = min(tile, S)
    assert u_col % P == 0 and z_col % P == 0 and T % POOL_HALO == 0
    return pl.pallas_call(
        functools.partial(_pool_body, group_dim=Cg),
        grid=(S // T,),
        in_specs=[
            pl.BlockSpec((T, P), lambda i: (i, u_col // P)),
            pl.BlockSpec((POOL_HALO, P), lambda i: (jnp.maximum(i * (T // POOL_HALO) - 1, 0), u_col // P)),
            pl.BlockSpec((T, P), lambda i: (i, z_col // P)),
            pl.BlockSpec((G, Cg, Cg), lambda i: (0, 0, 0)),
            pl.BlockSpec((1, P), lambda i: (0, 0)),
        ],
        out_specs=pl.BlockSpec((T, P), lambda i: (i, 0)),
        out_shape=jax.ShapeDtypeStruct((S, P), BF16),
        compiler_params=_params("parallel"),
        name="multiscale_pool",
    )(proj, proj, proj, pool_w.astype(F32), pool_scale.reshape(1, P).astype(F32))


def _front_body(xm_ref, halo_ref, cw_ref, cb_ref, wq_ref, wk_ref, wv_ref, wif_ref, bif_ref,
                xc_ref, q_ref, k_ref, v_ref, g_ref):
    T, TC = xm_ref.shape
    i = pl.program_id(0)
    j = pl.program_id(1)
    xm = xm_ref[...]
    halo = halo_ref[...] * jnp.where(i > 0, 1.0, 0.0)
    e = jnp.concatenate([halo, xm], axis=0)
    conv = cb_ref[...]
    for t in reversed(range(CONV_WIDTH)):
        off = CONV_HALO - (CONV_WIDTH - 1) + t
        conv = conv + e[off:off + T, :] * cw_ref[t:t + 1, :]
    xcb = _silu(conv).astype(BF16)
    xc_ref[...] = xcb
    xmb = xm.astype(BF16)
    gates = jnp.zeros(g_ref.shape, F32)
    W = wq_ref.shape[-1]
    nt = TC // W
    for n in range(nt):
        cols = slice(n * W, (n + 1) * W)
        t = j * nt + n
        wqk = jnp.concatenate([wq_ref[t], wk_ref[t]], axis=1)
        yqk = jnp.dot(xcb[:, cols], wqk, preferred_element_type=F32).astype(BF16)
        q_ref[:, cols] = yqk[:, :W]
        k_ref[:, cols] = yqk[:, W:]
        wif_qk = jnp.concatenate([wif_ref[0, t], wif_ref[1, t]], axis=0)
        yv = jnp.dot(xmb[:, cols], wv_ref[t], preferred_element_type=F32).astype(BF16)
        v_ref[:, cols] = yv
        gates = (gates + jnp.dot(yqk, wif_qk, preferred_element_type=F32)
                 + jnp.dot(yv, wif_ref[2, t], preferred_element_type=F32))

    @pl.when(j == 0)
    def _():
        g_ref[...] = bif_ref[...] + gates

    @pl.when(j > 0)
    def _():
        g_ref[...] += gates


def _block_diag_dense(w, width):
    nb, bs, _ = w.shape
    per = width // bs
    wg = w.reshape(nb // per, per, bs, bs)
    eye = jnp.eye(per, dtype=w.dtype)
    dense = jnp.einsum('gpcd,pq->gpcqd', wg, eye)
    return dense.reshape(nb // per, width, width).astype(BF16)


def _mlstm_front(up, conv_w, conv_b, wq, wk, wv, w_if, b_if, tile=512, tcol=1024):
    S = up.shape[0]
    E = conv_w.shape[1]
    T, TC = min(tile, S), min(tcol, E)
    NG = w_if.shape[1]
    W = min(QKV_TILE, TC)
    wdense = [_block_diag_dense(w, W) for w in (wq, wk, wv)]
    wif = w_if.reshape(3, E // W, W, NG).astype(BF16)
    row = lambda: pl.BlockSpec((T, TC), lambda i, j: (i, j))
    whole = lambda shape: pl.BlockSpec(shape, lambda i, j: (0,) * len(shape),
                                       pipeline_mode=pl.Buffered(1))
    return pl.pallas_call(
        _front_body,
        grid=(S // T, E // TC),
        in_specs=[
            row(),
            pl.BlockSpec((CONV_HALO, TC), lambda i, j: (jnp.maximum(i * (T // CONV_HALO) - 1, 0), j)),
            pl.BlockSpec((CONV_WIDTH, TC), lambda i, j: (0, j)),
            pl.BlockSpec((1, TC), lambda i, j: (0, j)),
            whole(wdense[0].shape), whole(wdense[1].shape), whole(wdense[2].shape),
            whole(wif.shape),
            pl.BlockSpec((1, NG), lambda i, j: (0, 0)),
        ],
        out_specs=[row(), row(), row(), row(), pl.BlockSpec((T, NG), lambda i, j: (i, 0))],
        out_shape=[jax.ShapeDtypeStruct((S, E), BF16)] * 4 + [jax.ShapeDtypeStruct((S, NG), F32)],
        compiler_params=_params("parallel", "arbitrary"),
        name="mlstm_front",
    )(up, up, conv_w.astype(F32), conv_b.reshape(1, E).astype(F32), *wdense, wif,
      b_if.reshape(1, NG).astype(F32))


def _mlstm_body(q_ref, k_ref, v_ref, ig_ref, fg_ref, op_ref, xc_ref, z_ref, gn_ref, sk_ref,
                wf_ref, o_ref, wb_ref, c_ref, cb_ref, n_ref, m_ref, *, cast_blocks):
    L = q_ref.shape[0]
    G, DK, _ = c_ref.shape
    c = pl.program_id(1)
    _side_cast(wf_ref, wb_ref, pl.program_id(0) * pl.num_programs(1) + c, cast_blocks)

    @pl.when(c == 0)
    def _():
        c_ref[...] = jnp.zeros_like(c_ref)
        cb_ref[...] = jnp.zeros_like(cb_ref)
        n_ref[...] = jnp.zeros_like(n_ref)
        m_ref[...] = jnp.full(m_ref.shape, -1e30, F32)

    ri = lax.broadcasted_iota(jnp.int32, (L, L), 0)
    cj = lax.broadcasted_iota(jnp.int32, (L, L), 1)
    lane8 = lax.broadcasted_iota(jnp.int32, (8, L), 1)
    scale = DK ** -0.5

    for hh in range(G):
        cols = slice(hh * DK, (hh + 1) * DK)
        i_row = ig_ref[hh, 0]
        f_row = fg_ref[hh, 0]
        lf_row = jnp.minimum(f_row, 0.0) - jnp.log1p(jnp.exp(-jnp.abs(f_row)))
        b8 = jnp.broadcast_to(lf_row, (8, L))
        sh = 1
        while sh < L:
            b8 = b8 + jnp.where(lane8 >= sh, pltpu.roll(b8, sh, 1), 0.0)
            sh *= 2
        b_row = b8[0:1, :]
        stacked = jnp.where(ri == 0, jnp.broadcast_to(b_row, (L, L)),
                            jnp.where(ri == 1, jnp.broadcast_to(i_row, (L, L)), 0.0))
        stacked_t = stacked.T
        b_col = stacked_t[:, 0:1]
        i_col = stacked_t[:, 1:2]

        m_prev = m_ref[hh, 0:1, 0:1]
        log_d = jnp.where(cj <= ri, b_col - b_row + i_row, -jnp.inf)
        log_inter = b_col + m_prev
        m_t = jnp.maximum(jnp.max(log_d, axis=-1, keepdims=True), log_inter)
        dmat = jnp.exp(log_d - m_t) * scale
        g = jnp.exp(log_inter - m_t)

        q = q_ref[:, cols]
        k = k_ref[:, cols]
        v = v_ref[:, cols]
        s = lax.dot_general(q, k, (((1,), (1,)), ((), ())), preferred_element_type=F32) * dmat
        inter = jnp.dot(q, cb_ref[hh], preferred_element_type=F32)
        num = jnp.dot(s.astype(BF16), v, preferred_element_type=F32) + g * inter
        qn = jnp.sum(q.astype(F32) * n_ref[hh], axis=-1, keepdims=True)
        den = jnp.sum(s, axis=-1, keepdims=True) + g * qn
        hc = num / jnp.maximum(jnp.abs(den), jnp.exp(-m_t))

        m_new = m_t[L - 1:L, :]
        b_last = b_col[L - 1:L, :]
        decay = jnp.exp(b_last + m_prev - m_new)
        w_col = jnp.exp(b_last - b_col + i_col - m_new) * scale
        vw = (v.astype(F32) * w_col).astype(BF16)
        upd = lax.dot_general(k, vw, (((0,), (0,)), ((), ())), preferred_element_type=F32)
        c_new = upd + c_ref[hh] * decay
        c_ref[hh] = c_new
        cb_ref[hh] = c_new.astype(BF16)
        n_ref[hh] = n_ref[hh] * decay + jnp.sum(k.astype(F32) * w_col, axis=0, keepdims=True)
        m_ref[hh] = jnp.broadcast_to(m_new, m_ref.shape[1:])

        mu = jnp.mean(hc, axis=-1, keepdims=True)
        ctr = hc - mu
        var = jnp.mean(ctr * ctr, axis=-1, keepdims=True)
        hn = ctr * lax.rsqrt(var + NORM_EPS)
        cell = (1.0 / (1.0 + jnp.exp(-op_ref[:, cols]))) * (hn * gn_ref[:, cols])
        o_ref[:, cols] = ((cell + sk_ref[:, cols] * xc_ref[:, cols].astype(F32))
                          * _silu(z_ref[:, cols])).astype(o_ref.dtype)


def _mlstm(q, k, v, gates, up, xc, gn, skip, w_stack, layer):
    S, E = q.shape
    H, L, G = MLSTM_HEADS, MLSTM_CHUNK, MLSTM_GROUP
    DH = E // H
    NCH = S // L
    HG = H // G
    w_in_spec, w_out_spec, cast_blocks = _side_cast_specs(w_stack, layer, HG * NCH,
                                                          lambda h, c: h * NCH + c)
    gt = gates.T.reshape(2 * H, NCH, 1, L)
    blk = lambda col0: pl.BlockSpec((L, G * DH), lambda h, c: (c, col0 + h))
    vec = pl.BlockSpec((1, G * DH), lambda h, c: (0, h))
    return pl.pallas_call(
        functools.partial(_mlstm_body, cast_blocks=cast_blocks),
        grid=(HG, NCH),
        in_specs=[
            blk(0), blk(0), blk(0),
            pl.BlockSpec((G, 1, 1, L), lambda h, c: (h, c, 0, 0)),
            pl.BlockSpec((G, 1, 1, L), lambda h, c: (HG + h, c, 0, 0)),
            blk(2 * HG), blk(0), blk(HG),
            vec, vec,
            w_in_spec,
        ],
        out_specs=[blk(0), w_out_spec],
        out_shape=[jax.ShapeDtypeStruct((S, E), BF16), jax.ShapeDtypeStruct(w_stack.shape[1:], BF16)],
        scratch_shapes=[
            pltpu.VMEM((G, DH, DH), F32),
            pltpu.VMEM((G, DH, DH), BF16),
            pltpu.VMEM((G, 1, DH), F32),
            pltpu.VMEM((G, 8, LANES), F32),
        ],
        compiler_params=_params("arbitrary", "arbitrary"),
        name="mlstm_chunkwise",
    )(q, k, v, gt, gt, up, xc, up, gn.reshape(1, E).astype(F32), skip.reshape(1, E).astype(F32),
      w_stack)


def _even_layer(h, xg, r, j, rel_bias, w_in, q_gain, k_gain, pool_w, pool_scale, w_out, next_gain):
    mix = w_out.shape[1]
    pool_width = pool_w.shape[0] * pool_w.shape[1]
    attn_width = mix - pool_width
    n_heads = attn_width // ATTN_HEAD_DIM
    u_col = 3 * attn_width
    z_col = u_col + pool_width
    proj = _matmul([xg], w_in, j, row_scale=r, tm=IN_PROJ_ROWS)
    attn, w_out_bf16 = _attention(proj, rel_bias, q_gain, k_gain, n_heads,
                                  z_col // ATTN_HEAD_DIM, w_out, j)
    pool = _pool(proj, pool_w, pool_scale, u_col, z_col + attn_width)
    return _matmul([attn, pool], w_out_bf16, j, residual=h, next_gain=next_gain,
                   tn=OUT_PROJ_COLS, a_buffers=2)


def _odd_layer(h, xg, r, j, w_up, conv_w, conv_b, wq, wk, wv, w_if, b_if, gn, skip, w_down,
               next_gain):
    up = _matmul([xg], w_up, j, row_scale=r, tm=IN_PROJ_ROWS)
    xc, q, k, v, gates = _mlstm_front(up, conv_w, conv_b, wq, wk, wv, w_if, b_if)
    out, w_down_bf16 = _mlstm(q, k, v, gates, up, xc, gn, skip, w_down, j)
    return _matmul([out], w_down_bf16, j, residual=h, next_gain=next_gain,
                   tn=OUT_PROJ_COLS, a_buffers=2)


def kernel(x, rel_bias, e_norm, e_w_in, e_q_gain, e_k_gain, e_pool_w, e_pool_scale, e_w_out,
           o_norm, o_w_up, o_conv_w, o_conv_b, o_wq, o_wk, o_wv, o_w_if, o_b_if, o_gn, o_skip,
           o_w_down):
    B, S, D = x.shape
    depth = e_norm.shape[0] + o_norm.shape[0]
    outs = []
    for b in range(B):
        h = x[b]
        gains = [(e_norm if layer % 2 == 0 else o_norm)[layer // 2] for layer in range(depth)]
        xg, r = _norm_factors(h, gains[0])
        for layer in range(depth):
            j = layer // 2
            next_gain = gains[layer + 1] if layer + 1 < depth else None
            if layer % 2 == 0:
                res = _even_layer(h, xg, r, j, rel_bias, e_w_in, e_q_gain[j], e_k_gain[j],
                                  e_pool_w[j], e_pool_scale[j], e_w_out, next_gain)
            else:
                res = _odd_layer(h, xg, r, j, o_w_up, o_conv_w[j], o_conv_b[j], o_wq[j], o_wk[j],
                                 o_wv[j], o_w_if[j], o_b_if[j], o_gn[j], o_skip[j], o_w_down,
                                 next_gain)
            h, xg, r = res if next_gain is not None else (res, None, None)
        outs.append(h)
    return jnp.stack(outs)
```

```python
import functools

import numpy as np
import jax
import jax.numpy as jnp
from jax import lax
from jax.experimental import pallas as pl
from jax.experimental.pallas import tpu as pltpu

F32 = jnp.float32
BF16 = jnp.bfloat16

NORM_EPS = 1e-6
RESULT_ROWS = 512
IN_PROJ_ROWS = 2048
OUT_PROJ_COLS = 256
LOG2E = 1.4426950408889634
LANES = 128
ATTN_HEAD_DIM = 128
ATTN_BLOCK = 128
DILATIONS = (1, 4, 16)
ATTN_CHUNK = ATTN_BLOCK * DILATIONS[-1]
POOL_WINDOWS = (2, 4, 8, 16)
POOL_HALO = 16
REL_BUCKETS = 32
REL_MAX_DIST = 2048
MLSTM_HEADS = 8
MLSTM_CHUNK = 256
MLSTM_GROUP = 2
CONV_WIDTH = 4
CONV_HALO = 8
QKV_TILE = 128
VMEM_LIMIT = 56 * 1024 * 1024


def _params(*sem):
    return pltpu.CompilerParams(dimension_semantics=sem, vmem_limit_bytes=VMEM_LIMIT)


def _silu(x):
    return x * (1.0 / (1.0 + jnp.exp(-x)))


def _mm_body(*refs, n_a, has_res, has_scale, has_next, n_cols):
    a_refs = refs[:n_a]
    w_refs = refs[n_a:2 * n_a]
    p = 2 * n_a
    res_ref = refs[p] if has_res else None
    p += int(has_res)
    scale_ref = refs[p] if has_scale else None
    p += int(has_scale)
    gain_ref = refs[p] if has_next else None
    p += int(has_next)
    o_ref = refs[p]
    acc = None
    for a_ref, w_ref in zip(a_refs, w_refs):
        if w_ref.dtype == BF16:
            d = jnp.dot(a_ref[...], w_ref[...], preferred_element_type=F32)
            acc = d if acc is None else acc + d
            continue
        wb = w_ref[...].astype(BF16)
        rg = min(RESULT_ROWS, a_ref.shape[0])
        d = jnp.concatenate([jnp.dot(a_ref[r0:r0 + rg, :], wb, preferred_element_type=F32)
                             for r0 in range(0, a_ref.shape[0], rg)], axis=0)
        acc = d if acc is None else d + acc
    if has_scale:
        acc = acc * scale_ref[:, 0:1]
    if has_res:
        acc = res_ref[...] + acc
    o_ref[...] = acc.astype(o_ref.dtype)
    if has_next:
        xg_ref, r_ref, ssq_ref = refs[p + 1], refs[p + 2], refs[p + 3]
        j = pl.program_id(1)
        xg_ref[...] = (acc * gain_ref[...]).astype(BF16)
        part = jnp.broadcast_to(jnp.sum(acc * acc, axis=-1, keepdims=True), ssq_ref.shape)

        @pl.when(j == 0)
        def _():
            ssq_ref[...] = part

        @pl.when(j > 0)
        def _():
            ssq_ref[...] += part

        @pl.when(j == pl.num_programs(1) - 1)
        def _():
            r_ref[...] = lax.rsqrt(ssq_ref[...] * (1.0 / n_cols) + NORM_EPS)


def _matmul(a_list, w_stack, layer, residual=None, row_scale=None, next_gain=None,
            out_dtype=F32, tm=1024, tn=512, a_buffers=1):
    M = a_list[0].shape[0]
    K, N = w_stack.shape[-2:]
    tm, tn = min(tm, M), min(tn, N)
    assert M % tm == 0 and N % tn == 0 and sum(a.shape[1] for a in a_list) == K
    in_specs, w_specs, row0 = [], [], 0
    for a in a_list:
        kp = a.shape[1]
        assert row0 % kp == 0
        in_specs.append(pl.BlockSpec((tm, kp), lambda i, j: (i, 0),
                                     pipeline_mode=pl.Buffered(a_buffers)))
        if w_stack.ndim == 3:
            w_specs.append(pl.BlockSpec((None, kp, tn), lambda i, j, rb=row0 // kp: (layer, rb, j)))
        else:
            w_specs.append(pl.BlockSpec((kp, tn), lambda i, j, rb=row0 // kp: (rb, j)))
        row0 += kp
    in_specs += w_specs
    args = list(a_list) + [w_stack] * len(a_list)
    tile = pl.BlockSpec((tm, tn), lambda i, j: (i, j))
    rows = pl.BlockSpec((tm, LANES), lambda i, j: (i, 0))
    if residual is not None:
        in_specs.append(tile)
        args.append(residual)
    if row_scale is not None:
        in_specs.append(rows)
        args.append(row_scale)
    out_specs, out_shape, scratch = tile, jax.ShapeDtypeStruct((M, N), out_dtype), []
    if next_gain is not None:
        in_specs.append(pl.BlockSpec((1, tn), lambda i, j: (0, j)))
        args.append(next_gain.reshape(1, N).astype(F32))
        out_specs = [tile, tile, rows]
        out_shape = [out_shape, jax.ShapeDtypeStruct((M, N), BF16), jax.ShapeDtypeStruct((M, LANES), F32)]
        scratch = [pltpu.VMEM((tm, LANES), F32)]
    return pl.pallas_call(
        functools.partial(_mm_body, n_a=len(a_list), has_res=residual is not None,
                          has_scale=row_scale is not None, has_next=next_gain is not None, n_cols=N),
        grid=(M // tm, N // tn),
        in_specs=in_specs,
        out_specs=out_specs,
        out_shape=out_shape,
        scratch_shapes=scratch,
        compiler_params=_params("arbitrary", "arbitrary"),
        name="matmul",
    )(*args)


def _side_cast_specs(w_stack, layer, n_steps, step_of):
    K, N = w_stack.shape[1:]
    n_blocks = 1 << (n_steps.bit_length() - 1)
    while K % n_blocks or (K // n_blocks) % 16:
        n_blocks //= 2
    rb = K // n_blocks
    idx = lambda *g: jnp.minimum(step_of(*g), n_blocks - 1)
    return (pl.BlockSpec((None, rb, N), lambda *g: (layer, idx(*g), 0)),
            pl.BlockSpec((rb, N), lambda *g: (idx(*g), 0)), n_blocks)


def _side_cast(wf_ref, wb_ref, step, n_blocks):
    @pl.when(step < n_blocks)
    def _():
        wb_ref[...] = wf_ref[...].astype(BF16)


def _norm_factors_body(x_ref, g_ref, xg_ref, r_ref):
    x = x_ref[...]
    xg_ref[...] = (x * g_ref[...]).astype(BF16)
    ms = jnp.mean(x * x, axis=-1, keepdims=True)
    r_ref[...] = jnp.broadcast_to(lax.rsqrt(ms + NORM_EPS), r_ref.shape)


def _norm_factors(x, gain, tm=256):
    M, D = x.shape
    tm = min(tm, M)
    return pl.pallas_call(
        _norm_factors_body,
        grid=(M // tm,),
        in_specs=[pl.BlockSpec((tm, D), lambda i: (i, 0)),
                  pl.BlockSpec((1, D), lambda i: (0, 0))],
        out_specs=[pl.BlockSpec((tm, D), lambda i: (i, 0)),
                   pl.BlockSpec((tm, LANES), lambda i: (i, 0))],
        out_shape=[jax.ShapeDtypeStruct((M, D), BF16), jax.ShapeDtypeStruct((M, LANES), F32)],
        compiler_params=_params("parallel"),
        name="norm_factors",
    )(x, gain.reshape(1, D).astype(F32))


def _t5_bucket(dist):
    max_exact = REL_BUCKETS // 2
    safe = np.maximum(dist, 1).astype(np.float32)
    large = max_exact + (np.log(safe / max_exact) / np.log(REL_MAX_DIST / max_exact)
                         * (REL_BUCKETS - max_exact)).astype(np.int32)
    large = np.minimum(large, REL_BUCKETS - 1)
    return np.where(dist < max_exact, dist, large).astype(np.int32)


def _attn_bucket_table():
    B = ATTN_BLOCK
    rel = B - np.arange(2 * B)
    band = (rel >= 0) & (rel <= B)
    rows = [np.where(band, _t5_bucket(np.clip(rel, 0, None) * d), -1) for d in DILATIONS]
    return np.broadcast_to(np.stack(rows)[:, None, :], (len(DILATIONS), 8, 2 * B)).astype(np.int32)


def _attn_body(relb_ref, bkt_ref, q_ref, k_ref, v_ref, z_ref, qg_ref, kg_ref, wf_ref,
               o_ref, wb_ref, bias_ref, qn_ref, q4_ref, kn_ref, k4_ref, vn_ref, v4_ref,
               o1_ref, m1_ref, l1_ref, o2_ref, m2_ref, l2_ref, o3_ref, m3_ref, l3_ref, out_ref,
               *, cast_blocks):
    B = ATTN_BLOCK
    C = ATTN_CHUNK
    R4 = DILATIONS[1]
    Q = C // R4
    h = pl.program_id(0)
    c = pl.program_id(1)
    n_dil = len(DILATIONS)
    cur = c % 2
    prv = 1 - cur

    @pl.when(c == 0)
    def _():
        col = lax.broadcasted_iota(jnp.int32, (B, 2 * B), 1)
        for t in range(n_dil):
            bkt = bkt_ref[t]
            row = jnp.full(bkt.shape, -jnp.inf, F32)
            for b in range(REL_BUCKETS):
                row = jnp.where(bkt == b, relb_ref[b, h] * LOG2E, row)
            bias = pltpu.roll(jnp.broadcast_to(row[0:1, :], (B, 2 * B)), 0, 1,
                              stride=1, stride_axis=0)
            bias_ref[t] = bias
            bias_ref[t + n_dil] = jnp.where(col >= B, bias, -jnp.inf)

    @pl.when(jnp.logical_and(h == 0, c == 0))
    def _():
        kn_ref[1] = jnp.zeros(kn_ref.shape[1:], F32)
        vn_ref[1] = jnp.zeros(vn_ref.shape[1:], F32)
        k4_ref[1] = jnp.zeros(k4_ref.shape[1:], F32)
        v4_ref[1] = jnp.zeros(v4_ref.shape[1:], F32)

    def _norm(x, g):
        ms = jnp.mean(x * x, axis=-1, keepdims=True)
        return x * lax.rsqrt(ms + NORM_EPS) * g

    qn_ref[...] = _norm(q_ref[...], qg_ref[...]) * (ATTN_HEAD_DIM ** -0.5 * LOG2E)
    kn_ref[cur] = _norm(k_ref[...], kg_ref[...])
    vn_ref[cur] = v_ref[...]
    for r4 in range(R4):
        rows = pl.ds(r4, Q, stride=R4)
        q4_ref[r4] = qn_ref[rows, :]
        k4_ref[cur, r4] = kn_ref[cur, rows, :]
        v4_ref[cur, r4] = vn_ref[cur, rows, :]

    first_chunk = jnp.where(c == 0, 1, 0)

    def block(q, k, v, bias):
        s = lax.dot_general(q.astype(BF16), k.astype(BF16), (((1,), (1,)), ((), ())),
                            preferred_element_type=F32) + bias
        m = jnp.max(s, axis=-1, keepdims=True)
        p = jnp.exp2(s - m).astype(BF16)
        v_aug = jnp.concatenate([v.astype(BF16), jnp.ones((2 * B, LANES), BF16)], axis=1)
        o_aug = jnp.dot(p, v_aug, preferred_element_type=F32)
        return o_aug[:, :ATTN_HEAD_DIM], m, o_aug[:, ATTN_HEAD_DIM:]

    def bcast(x):
        return jnp.broadcast_to(x, (B, LANES))

    for b in range(C // B):
        rb = slice(b * B, (b + 1) * B)
        if b == 0:
            ka, va = kn_ref[prv, C - B:C, :], vn_ref[prv, C - B:C, :]
            bias = bias_ref[n_dil * first_chunk]
        else:
            ka, va = kn_ref[cur, (b - 1) * B:b * B, :], vn_ref[cur, (b - 1) * B:b * B, :]
            bias = bias_ref[0]
        k = jnp.concatenate([ka, kn_ref[cur, rb, :]], axis=0)
        v = jnp.concatenate([va, vn_ref[cur, rb, :]], axis=0)
        o, m, l = block(qn_ref[rb, :], k, v, bias)
        o1_ref[rb, :] = o
        m1_ref[rb, :] = bcast(m)
        l1_ref[rb, :] = l

    for sub in range(Q // B):
        rb = slice(sub * B, (sub + 1) * B)
        bias = bias_ref[1 + n_dil * first_chunk] if sub == 0 else bias_ref[1]
        for r4 in range(R4):
            if sub == 0:
                ka, va = k4_ref[prv, r4, Q - B:Q, :], v4_ref[prv, r4, Q - B:Q, :]
            else:
                ra = slice((sub - 1) * B, sub * B)
                ka, va = k4_ref[cur, r4, ra, :], v4_ref[cur, r4, ra, :]
            k = jnp.concatenate([ka, k4_ref[cur, r4, rb, :]], axis=0)
            v = jnp.concatenate([va, v4_ref[cur, r4, rb, :]], axis=0)
            o, m, l = block(q4_ref[r4, rb, :], k, v, bias)
            o2_ref[r4, rb, :] = o
            m2_ref[r4, rb, :] = bcast(m)
            l2_ref[r4, rb, :] = l

    bias = bias_ref[2 + n_dil * first_chunk]
    for o4 in range(Q // B):
        rows = pl.ds(o4, B, stride=R4)
        for r4 in range(R4):
            k = jnp.concatenate([k4_ref[prv, r4, rows, :], k4_ref[cur, r4, rows, :]], axis=0)
            v = jnp.concatenate([v4_ref[prv, r4, rows, :], v4_ref[cur, r4, rows, :]], axis=0)
            o, m, l = block(q4_ref[r4, rows, :], k, v, bias)
            o3_ref[r4, rows, :] = o
            m3_ref[r4, rows, :] = bcast(m)
            l3_ref[r4, rows, :] = l

    def body_merge(sub, carry):
        r0 = pl.multiple_of(sub * B, B)
        for r4 in range(R4):
            nat = pl.ds(sub * (B * R4) + r4, B, stride=R4)
            m1, m2, m3 = m1_ref[nat, :], m2_ref[r4, pl.ds(r0, B), :], m3_ref[r4, pl.ds(r0, B), :]
            mx = jnp.maximum(jnp.maximum(m1, m2), m3)
            w1, w2, w3 = jnp.exp2(m1 - mx), jnp.exp2(m2 - mx), jnp.exp2(m3 - mx)
            num = (o1_ref[nat, :] * w1 + o2_ref[r4, pl.ds(r0, B), :] * w2
                   + o3_ref[r4, pl.ds(r0, B), :] * w3)
            den = (l1_ref[nat, :] * w1 + l2_ref[r4, pl.ds(r0, B), :] * w2
                   + l3_ref[r4, pl.ds(r0, B), :] * w3)
            out_ref[nat, :] = num / den
        return carry

    lax.fori_loop(0, Q // B, body_merge, 0)

    o_ref[...] = (out_ref[...] * _silu(z_ref[...])).astype(o_ref.dtype)
    _side_cast(wf_ref, wb_ref, h * pl.num_programs(1) + c, cast_blocks)


def _attention(proj, rel_bias, q_gain, k_gain, n_heads, z_col, w_stack, layer):
    S = proj.shape[0]
    C, B, Dh = ATTN_CHUNK, ATTN_BLOCK, ATTN_HEAD_DIM
    assert S % C == 0
    H = n_heads
    NC = S // C
    w_in_spec, w_out_spec, cast_blocks = _side_cast_specs(w_stack, layer, H * NC,
                                                          lambda h, c: h * NC + c)
    bkt = jnp.asarray(_attn_bucket_table())
    R4 = DILATIONS[1]
    blk = lambda f: pl.BlockSpec((C, Dh), f)
    nat = pltpu.VMEM((C, Dh), F32)
    mod4 = pltpu.VMEM((R4, C // R4, Dh), F32)
    return pl.pallas_call(
        functools.partial(_attn_body, cast_blocks=cast_blocks),
        grid=(H, NC),
        in_specs=[
            pl.BlockSpec(memory_space=pltpu.SMEM),
            pl.BlockSpec(bkt.shape, lambda h, c: (0, 0, 0)),
            blk(lambda h, c: (c, h)),
            blk(lambda h, c: (c, H + h)),
            blk(lambda h, c: (c, 2 * H + h)),
            blk(lambda h, c: (c, z_col + h)),
            pl.BlockSpec((1, Dh), lambda h, c: (0, 0)),
            pl.BlockSpec((1, Dh), lambda h, c: (0, 0)),
            w_in_spec,
        ],
        out_specs=[blk(lambda h, c: (c, h)), w_out_spec],
        out_shape=[jax.ShapeDtypeStruct((S, H * Dh), BF16),
                   jax.ShapeDtypeStruct(w_stack.shape[1:], BF16)],
        scratch_shapes=[
            pltpu.VMEM((2 * len(DILATIONS), B, 2 * B), F32),
            nat, mod4,
            pltpu.VMEM((2, C, Dh), F32), pltpu.VMEM((2, R4, C // R4, Dh), F32),
            pltpu.VMEM((2, C, Dh), F32), pltpu.VMEM((2, R4, C // R4, Dh), F32),
            nat, nat, nat,
            mod4, mod4, mod4,
            mod4, mod4, mod4,
            nat,
        ],
        compiler_params=_params("arbitrary", "arbitrary"),
        name="dilated_attention",
    )(rel_bias.astype(F32), bkt, proj, proj, proj, proj,
      q_gain.reshape(1, Dh).astype(F32), k_gain.reshape(1, Dh).astype(F32), w_stack)


def _pool_body(u_ref, halo_ref, z_ref, w_ref, sc_ref, o_ref, *, group_dim):
    T = u_ref.shape[0]
    i = pl.program_id(0)
    halo_on = jnp.where(i > 0, 1.0, 0.0)
    pos = (i * T + lax.broadcasted_iota(jnp.int32, (T, 1), 0) + 1).astype(F32)
    for g, w in enumerate(POOL_WINDOWS):
        cols = slice(g * group_dim, (g + 1) * group_dim)
        x = u_ref[:, cols]
        e = jnp.concatenate([halo_ref[:, cols] * halo_on, x], axis=0)
        width = 1
        while width < w:
            e = e[width:, :] + e[:-width, :]
            width *= 2
        off = POOL_HALO - (w - 1)
        win = e[off:off + T, :]
        y = win / jnp.minimum(pos, float(w)) - x
        yp = jnp.dot(y.astype(BF16), w_ref[g].astype(BF16), preferred_element_type=F32)
        o_ref[:, cols] = (yp * sc_ref[:, cols] * _silu(z_ref[:, cols])).astype(o_ref.dtype)


def _pool(proj, pool_w, pool_scale, u_col, z_col, tile=256):
    S = proj.shape[0]
    G, Cg, _ = pool_w.shape
    P = G * Cg
    T = min(tile, S)
    assert u_col % P == 0 and z_col % P == 0 and T % POOL_HALO == 0
    return pl.pallas_call(
        functools.partial(_pool_body, group_dim=Cg),
        grid=(S // T,),
        in_specs=[
            pl.BlockSpec((T, P), lambda i: (i, u_col // P)),
            pl.BlockSpec((POOL_HALO, P), lambda i: (jnp.maximum(i * (T // POOL_HALO) - 1, 0), u_col // P)),
            pl.BlockSpec((T, P), lambda i: (i, z_col // P)),
            pl.BlockSpec((G, Cg, Cg), lambda i: (0, 0, 0)),
            pl.BlockSpec((1, P), lambda i: (0, 0)),
        ],
        out_specs=pl.BlockSpec((T, P), lambda i: (i, 0)),
        out_shape=jax.ShapeDtypeStruct((S, P), BF16),
        compiler_params=_params("parallel"),
        name="multiscale_pool",
    )(proj, proj, proj, pool_w.astype(F32), pool_scale.reshape(1, P).astype(F32))


def _front_body(xm_ref, halo_ref, cw_ref, cb_ref, wq_ref, wk_ref, wv_ref, wif_ref, bif_ref,
                xc_ref, q_ref, k_ref, v_ref, g_ref):
    T, TC = xm_ref.shape
    i = pl.program_id(0)
    j = pl.program_id(1)
    xm = xm_ref[...]
    halo = halo_ref[...] * jnp.where(i > 0, 1.0, 0.0)
    e = jnp.concatenate([halo, xm], axis=0)
    conv = cb_ref[...]
    for t in reversed(range(CONV_WIDTH)):
        off = CONV_HALO - (CONV_WIDTH - 1) + t
        conv = conv + e[off:off + T, :] * cw_ref[t:t + 1, :]
    xcb = _silu(conv).astype(BF16)
    xc_ref[...] = xcb
    xmb = xm.astype(BF16)
    gates = jnp.zeros(g_ref.shape, F32)
    W = wq_ref.shape[-1]
    nt = TC // W
    for n in range(nt):
        cols = slice(n * W, (n + 1) * W)
        t = j * nt + n
        wqk = jnp.concatenate([wq_ref[t], wk_ref[t]], axis=1)
        yqk = jnp.dot(xcb[:, cols], wqk, preferred_element_type=F32).astype(BF16)
        q_ref[:, cols] = yqk[:, :W]
        k_ref[:, cols] = yqk[:, W:]
        wif_qk = jnp.concatenate([wif_ref[0, t], wif_ref[1, t]], axis=0)
        yv = jnp.dot(xmb[:, cols], wv_ref[t], preferred_element_type=F32).astype(BF16)
        v_ref[:, cols] = yv
        gates = (gates + jnp.dot(yqk, wif_qk, preferred_element_type=F32)
                 + jnp.dot(yv, wif_ref[2, t], preferred_element_type=F32))

    @pl.when(j == 0)
    def _():
        g_ref[...] = bif_ref[...] + gates

    @pl.when(j > 0)
    def _():
        g_ref[...] += gates


def _block_diag_dense(w, width):
    nb, bs, _ = w.shape
    per = width // bs
    wg = w.reshape(nb // per, per, bs, bs)
    eye = jnp.eye(per, dtype=w.dtype)
    dense = jnp.einsum('gpcd,pq->gpcqd', wg, eye)
    return dense.reshape(nb // per, width, width).astype(BF16)


def _mlstm_front(up, conv_w, conv_b, wq, wk, wv, w_if, b_if, tile=512, tcol=1024):
    S = up.shape[0]
    E = conv_w.shape[1]
    T, TC = min(tile, S), min(tcol, E)
    NG = w_if.shape[1]
    W = min(QKV_TILE, TC)
    wdense = [_block_diag_dense(w, W) for w in (wq, wk, wv)]
    wif = w_if.reshape(3, E // W, W, NG).astype(BF16)
    row = lambda: pl.BlockSpec((T, TC), lambda i, j: (i, j))
    whole = lambda shape: pl.BlockSpec(shape, lambda i, j: (0,) * len(shape),
                                       pipeline_mode=pl.Buffered(1))
    return pl.pallas_call(
        _front_body,
        grid=(S // T, E // TC),
        in_specs=[
            row(),
            pl.BlockSpec((CONV_HALO, TC), lambda i, j: (jnp.maximum(i * (T // CONV_HALO) - 1, 0), j)),
            pl.BlockSpec((CONV_WIDTH, TC), lambda i, j: (0, j)),
            pl.BlockSpec((1, TC), lambda i, j: (0, j)),
            whole(wdense[0].shape), whole(wdense[1].shape), whole(wdense[2].shape),
            whole(wif.shape),
            pl.BlockSpec((1, NG), lambda i, j: (0, 0)),
        ],
        out_specs=[row(), row(), row(), row(), pl.BlockSpec((T, NG), lambda i, j: (i, 0))],
        out_shape=[jax.ShapeDtypeStruct((S, E), BF16)] * 4 + [jax.ShapeDtypeStruct((S, NG), F32)],
        compiler_params=_params("parallel", "arbitrary"),
        name="mlstm_front",
    )(up, up, conv_w.astype(F32), conv_b.reshape(1, E).astype(F32), *wdense, wif,
      b_if.reshape(1, NG).astype(F32))


def _mlstm_body(q_ref, k_ref, v_ref, ig_ref, fg_ref, op_ref, xc_ref, z_ref, gn_ref, sk_ref,
                wf_ref, o_ref, wb_ref, c_ref, cb_ref, n_ref, m_ref, *, cast_blocks):
    L = q_ref.shape[0]
    G, DK, _ = c_ref.shape
    c = pl.program_id(1)
    _side_cast(wf_ref, wb_ref, pl.program_id(0) * pl.num_programs(1) + c, cast_blocks)

    @pl.when(c == 0)
    def _():
        c_ref[...] = jnp.zeros_like(c_ref)
        cb_ref[...] = jnp.zeros_like(cb_ref)
        n_ref[...] = jnp.zeros_like(n_ref)
        m_ref[...] = jnp.full(m_ref.shape, -1e30, F32)

    ri = lax.broadcasted_iota(jnp.int32, (L, L), 0)
    cj = lax.broadcasted_iota(jnp.int32, (L, L), 1)
    lane8 = lax.broadcasted_iota(jnp.int32, (8, L), 1)
    scale = DK ** -0.5

    for hh in range(G):
        cols = slice(hh * DK, (hh + 1) * DK)
        i_row = ig_ref[hh, 0]
        f_row = fg_ref[hh, 0]
        lf_row = jnp.minimum(f_row, 0.0) - jnp.log1p(jnp.exp(-jnp.abs(f_row)))
        b8 = jnp.broadcast_to(lf_row, (8, L))
        sh = 1
        while sh < L:
            b8 = b8 + jnp.where(lane8 >= sh, pltpu.roll(b8, sh, 1), 0.0)
            sh *= 2
        b_row = b8[0:1, :]
        stacked = jnp.where(ri == 0, jnp.broadcast_to(b_row, (L, L)),
                            jnp.where(ri == 1, jnp.broadcast_to(i_row, (L, L)), 0.0))
        stacked_t = stacked.T
        b_col = stacked_t[:, 0:1]
        i_col = stacked_t[:, 1:2]

        m_prev = m_ref[hh, 0:1, 0:1]
        log_d = jnp.where(cj <= ri, b_col - b_row + i_row, -jnp.inf)
        log_inter = b_col + m_prev
        m_t = jnp.maximum(jnp.max(log_d, axis=-1, keepdims=True), log_inter)
        dmat = jnp.exp(log_d - m_t) * scale
        g = jnp.exp(log_inter - m_t)

        q = q_ref[:, cols]
        k = k_ref[:, cols]
        v = v_ref[:, cols]
        s = lax.dot_general(q, k, (((1,), (1,)), ((), ())), preferred_element_type=F32) * dmat
        inter = jnp.dot(q, cb_ref[hh], preferred_element_type=F32)
        num = jnp.dot(s.astype(BF16), v, preferred_element_type=F32) + g * inter
        qn = jnp.sum(q.astype(F32) * n_ref[hh], axis=-1, keepdims=True)
        den = jnp.sum(s, axis=-1, keepdims=True) + g * qn
        hc = num / jnp.maximum(jnp.abs(den), jnp.exp(-m_t))

        m_new = m_t[L - 1:L, :]
        b_last = b_col[L - 1:L, :]
        decay = jnp.exp(b_last + m_prev - m_new)
        w_col = jnp.exp(b_last - b_col + i_col - m_new) * scale
        vw = (v.astype(F32) * w_col).astype(BF16)
        upd = lax.dot_general(k, vw, (((0,), (0,)), ((), ())), preferred_element_type=F32)
        c_new = upd + c_ref[hh] * decay
        c_ref[hh] = c_new
        cb_ref[hh] = c_new.astype(BF16)
        n_ref[hh] = n_ref[hh] * decay + jnp.sum(k.astype(F32) * w_col, axis=0, keepdims=True)
        m_ref[hh] = jnp.broadcast_to(m_new, m_ref.shape[1:])

        mu = jnp.mean(hc, axis=-1, keepdims=True)
        ctr = hc - mu
        var = jnp.mean(ctr * ctr, axis=-1, keepdims=True)
        hn = ctr * lax.rsqrt(var + NORM_EPS)
        cell = (1.0 / (1.0 + jnp.exp(-op_ref[:, cols]))) * (hn * gn_ref[:, cols])
        o_ref[:, cols] = ((cell + sk_ref[:, cols] * xc_ref[:, cols].astype(F32))
                          * _silu(z_ref[:, cols])).astype(o_ref.dtype)


def _mlstm(q, k, v, gates, up, xc, gn, skip, w_stack, layer):
    S, E = q.shape
    H, L, G = MLSTM_HEADS, MLSTM_CHUNK, MLSTM_GROUP
    DH = E // H
    NCH = S // L
    HG = H // G
    w_in_spec, w_out_spec, cast_blocks = _side_cast_specs(w_stack, layer, HG * NCH,
                                                          lambda h, c: h * NCH + c)
    gt = gates.T.reshape(2 * H, NCH, 1, L)
    blk = lambda col0: pl.BlockSpec((L, G * DH), lambda h, c: (c, col0 + h))
    vec = pl.BlockSpec((1, G * DH), lambda h, c: (0, h))
    return pl.pallas_call(
        functools.partial(_mlstm_body, cast_blocks=cast_blocks),
        grid=(HG, NCH),
        in_specs=[
            blk(0), blk(0), blk(0),
            pl.BlockSpec((G, 1, 1, L), lambda h, c: (h, c, 0, 0)),
            pl.BlockSpec((G, 1, 1, L), lambda h, c: (HG + h, c, 0, 0)),
            blk(2 * HG), blk(0), blk(HG),
            vec, vec,
            w_in_spec,
        ],
        out_specs=[blk(0), w_out_spec],
        out_shape=[jax.ShapeDtypeStruct((S, E), BF16), jax.ShapeDtypeStruct(w_stack.shape[1:], BF16)],
        scratch_shapes=[
            pltpu.VMEM((G, DH, DH), F32),
            pltpu.VMEM((G, DH, DH), BF16),
            pltpu.VMEM((G, 1, DH), F32),
            pltpu.VMEM((G, 8, LANES), F32),
        ],
        compiler_params=_params("arbitrary", "arbitrary"),
        name="mlstm_chunkwise",
    )(q, k, v, gt, gt, up, xc, up, gn.reshape(1, E).astype(F32), skip.reshape(1, E).astype(F32),
      w_stack)


def _even_layer(h, xg, r, j, rel_bias, w_in, q_gain, k_gain, pool_w, pool_scale, w_out, next_gain):
    mix = w_out.shape[1]
    pool_width = pool_w.shape[0] * pool_w.shape[1]
    attn_width = mix - pool_width
    n_heads = attn_width // ATTN_HEAD_DIM
    u_col = 3 * attn_width
    z_col = u_col + pool_width
    proj = _matmul([xg], w_in, j, row_scale=r, tm=IN_PROJ_ROWS)
    attn, w_out_bf16 = _attention(proj, rel_bias, q_gain, k_gain, n_heads,
                                  z_col // ATTN_HEAD_DIM, w_out, j)
    pool = _pool(proj, pool_w, pool_scale, u_col, z_col + attn_width)
    return _matmul([attn, pool], w_out_bf16, j, residual=h, next_gain=next_gain,
                   tn=OUT_PROJ_COLS, a_buffers=2)


def _odd_layer(h, xg, r, j, w_up, conv_w, conv_b, wq, wk, wv, w_if, b_if, gn, skip, w_down,
               next_gain):
    up = _matmul([xg], w_up, j, row_scale=r, tm=IN_PROJ_ROWS)
    xc, q, k, v, gates = _mlstm_front(up, conv_w, conv_b, wq, wk, wv, w_if, b_if)
    out, w_down_bf16 = _mlstm(q, k, v, gates, up, xc, gn, skip, w_down, j)
    return _matmul([out], w_down_bf16, j, residual=h, next_gain=next_gain,
                   tn=OUT_PROJ_COLS, a_buffers=2)


def kernel(x, rel_bias, e_norm, e_w_in, e_q_gain, e_k_gain, e_pool_w, e_pool_scale, e_w_out,
           o_norm, o_w_up, o_conv_w, o_conv_b, o_wq, o_wk, o_wv, o_w_if, o_b_if, o_gn, o_skip,
           o_w_down):
    B, S, D = x.shape
    depth = e_norm.shape[0] + o_norm.shape[0]
    outs = []
    for b in range(B):
        h = x[b]
        gains = [(e_norm if layer % 2 == 0 else o_norm)[layer // 2] for layer in range(depth)]
        xg, r = _norm_factors(h, gains[0])
        for layer in range(depth):
            j = layer // 2
            next_gain = gains[layer + 1] if layer + 1 < depth else None
            if layer % 2 == 0:
                res = _even_layer(h, xg, r, j, rel_bias, e_w_in, e_q_gain[j], e_k_gain[j],
                                  e_pool_w[j], e_pool_scale[j], e_w_out, next_gain)
            else:
                res = _odd_layer(h, xg, r, j, o_w_up, o_conv_w[j], o_conv_b[j], o_wq[j], o_wk[j],
                                 o_wv[j], o_w_if[j], o_b_if[j], o_gn[j], o_skip[j], o_w_down,
                                 next_gain)
            h, xg, r = res if next_gain is not None else (res, None, None)
        outs.append(h)
    return jnp.stack(outs)
```
